```python
import math
import jax, jax.numpy as jnp
from jax import lax
import numpy as np

D_MODEL = 2048
BATCH = 8
SEQ = 2048
DEPTH = 2
DEC_BATCH = 128
DEC_SEQ = 4
PAST_LEN = 8192
PAGE_SIZE = 128

NORM_EPS = 1e-6
NEG_INF = -1e30
MIX_WIDTH = D_MODEL
D_CONV = MIX_WIDTH // 2
CONV_WIDTH = 31
GLA_HEADS = 4
GLA_DV = MIX_WIDTH // 2
GLA_DK = GLA_DV // 2
GLA_HK = GLA_DK // GLA_HEADS
GLA_HV = GLA_DV // GLA_HEADS
GLA_RANK = 16
GLA_GATE_NORM = 16.0
GLA_CHUNK = 64
D_SGU = MIX_WIDTH // 2
SGU_GROUPS = 4
SGU_GD = D_SGU // SGU_GROUPS
SGU_CHUNK = 128
ATT_HD = 64
ATT_HEADS = (MIX_WIDTH // 2) // ATT_HD
ATT_KV_HEADS = 2
ATT_REP = ATT_HEADS // ATT_KV_HEADS
ATT_WIDTH = ATT_HEADS * ATT_HD
KV_WIDTH = ATT_KV_HEADS * ATT_HD
WINDOW = 128
D_FF = ((8 * D_MODEL + 3 * 256 - 1) // (3 * 256)) * 256
IN_AB = 2 * D_CONV + 2 * GLA_DK + 2 * GLA_DV + GLA_RANK
IN_CD = 2 * D_SGU + ATT_WIDTH + 2 * KV_WIDTH

kernel_name = 'hybrid_conv_gla_sgu_swa_step'


def split_cols(z, sizes):
    idx = np.cumsum(sizes)[:-1].tolist()
    return jnp.split(z, idx, axis=-1)


def rms_norm(x, g):
    xf = x.astype(jnp.float32)
    y = xf * lax.rsqrt(jnp.mean(xf * xf, axis=-1, keepdims=True) + NORM_EPS)
    return (y * g.astype(jnp.float32)).astype(x.dtype)


def layer_norm(x, g, b):
    xf = x.astype(jnp.float32)
    mu = jnp.mean(xf, axis=-1, keepdims=True)
    xc = xf - mu
    y = xc * lax.rsqrt(jnp.mean(xc * xc, axis=-1, keepdims=True) + NORM_EPS)
    return (y * g.astype(jnp.float32) + b.astype(jnp.float32)).astype(x.dtype)


def adaln(c, w, b):
    return jnp.split(jax.nn.silu(c) @ w + b, 6, axis=-1)


def modulate(x, g, shift, scale):
    return rms_norm(x, g) * (1.0 + scale[:, None, :]) + shift[:, None, :]


def swiglu_sublayer(x, shift, scale, gate, g, w_gate, w_up, w_down):
    h = modulate(x, g, shift, scale)
    return x + gate[:, None, :] * ((jax.nn.silu(h @ w_gate) * (h @ w_up)) @ w_down)


def conv_module(za, prev_rows, conv_w, conv_b, ln_g, ln_b):
    u = za[..., :D_CONV] * jax.nn.sigmoid(za[..., D_CONV:])
    u_ext = jnp.concatenate([prev_rows.astype(u.dtype), u], axis=1)
    y = lax.conv_general_dilated(u_ext, conv_w[:, None, :], window_strides=(1,), padding='VALID',
                                 dimension_numbers=('NWC', 'WIO', 'NWC'),
                                 feature_group_count=D_CONV) + conv_b
    y = jax.nn.silu(layer_norm(y, ln_g, ln_b))
    return y, u_ext[:, -(CONV_WIDTH - 1):]


def gla_chunked(q, k, v, log_a, s0):
    B, T, H, dk = q.shape
    L = math.gcd(T, GLA_CHUNK)
    n = T // L

    def to_chunks(z):
        return z.reshape(B, n, L, H, z.shape[-1]).transpose(1, 0, 3, 2, 4)

    mask = jnp.tril(jnp.ones((L, L), dtype=bool))

    def step(S, inp):
        qi, ki, vi, ai = inp
        b = jnp.cumsum(ai, axis=2)
        q_dec = qi * jnp.exp(b)
        att = jnp.einsum('bhtd,bhsd->bhts', q_dec, ki * jnp.exp(-b))
        att = jnp.where(mask, att, 0.0)
        o = jnp.einsum('bhts,bhsv->bhtv', att, vi) + jnp.einsum('bhtd,bhdv->bhtv', q_dec, S)
        b_last = b[:, :, -1:]
        S = S * jnp.exp(b_last[:, :, 0])[..., None] + jnp.einsum(
            'bhsd,bhsv->bhdv', ki * jnp.exp(b_last - b), vi)
        return S, o

    S, o = lax.scan(step, s0, (to_chunks(q), to_chunks(k), to_chunks(v), to_chunks(log_a)))
    o = o.transpose(1, 0, 3, 2, 4).reshape(B, T, H, v.shape[-1])
    return o, S


def gla_mixer(zq, zk, zv, zg, zr, s0, w_gate, b_gate, g_norm):
    B, T, _ = zq.shape
    f32 = jnp.float32

    def heads(z, d):
        return z.reshape(B, T, GLA_HEADS, d).astype(f32)

    log_a = jax.nn.log_sigmoid((zr @ w_gate + b_gate).astype(f32)) / GLA_GATE_NORM
    q = heads(zq, GLA_HK) * (GLA_HK ** -0.5)
    o, s_new = gla_chunked(q, heads(zk, GLA_HK), heads(zv, GLA_HV), heads(log_a, GLA_HK), s0.astype(f32))
    o = rms_norm(o, g_norm) * jax.nn.silu(heads(zg, GLA_HV))
    return o.reshape(B, T, GLA_DV).astype(zq.dtype), s_new.astype(s0.dtype)


def mix_ab(z, conv_prev, gla_prev, conv_w, conv_b, conv_ln_g, conv_ln_b, gla_w_gate, gla_b_gate, gla_g_norm):
    za, zq, zk, zv, zg, zr = split_cols(z, [2 * D_CONV, GLA_DK, GLA_DK, GLA_DV, GLA_DV, GLA_RANK])
    ya, conv_new = conv_module(za, conv_prev, conv_w, conv_b, conv_ln_g, conv_ln_b)
    yb, gla_new = gla_mixer(zq, zk, zv, zg, zr, gla_prev, gla_w_gate, gla_b_gate, gla_g_norm)
    return jnp.concatenate([ya, yb], axis=-1), conv_new, gla_new


def sgu_mixer(zu, zv, w_sp, b_sp, ln_g, ln_b):
    B, T, _ = zu.shape
    L = min(T, SGU_CHUNK)
    n = T // L
    u = jax.nn.gelu(zu)
    v = layer_norm(jax.nn.gelu(zv), ln_g, ln_b)
    w = jnp.where(jnp.tril(jnp.ones((SGU_CHUNK, SGU_CHUNK), dtype=bool)), w_sp, 0.0)[:, :L, :L]
    mixed = jnp.einsum('gts,bnsgc->bntgc', w, v.reshape(B, n, L, SGU_GROUPS, SGU_GD))
    mixed = mixed + b_sp[:, :L].T[:, :, None]
    return u * mixed.reshape(B, T, D_SGU), v


def alibi_slopes():
    s = 2.0 ** (-8.0 * np.arange(1, ATT_HEADS + 1) / ATT_HEADS)
    return jnp.asarray(s, dtype=jnp.float32).reshape(ATT_KV_HEADS, ATT_REP)


def sink_softmax(s, sink):
    m = jnp.maximum(jnp.max(s, axis=-1, keepdims=True), sink)
    p = jnp.exp(s - m)
    return p / (jnp.sum(p, axis=-1, keepdims=True) + jnp.exp(sink - m))


def swa_banded(q, k, v, sinks):
    B, T = q.shape[:2]
    W = WINDOW
    n = T // W
    f32 = jnp.float32

    def with_prev(z):
        zb = z.reshape(B, n, W, ATT_KV_HEADS, ATT_HD)
        prev = jnp.concatenate([jnp.zeros_like(zb[:, :1]), zb[:, :-1]], axis=1)
        return jnp.concatenate([prev, zb], axis=2)

    kb, vb = with_prev(k), with_prev(v)
    qb = q.reshape(B, n, W, ATT_KV_HEADS, ATT_REP, ATT_HD)
    kj = jnp.arange(2 * W)
    dist = jnp.arange(W)[:, None] - kj[None, :] + W
    valid = (dist >= 0) & (dist < W) & ((jnp.arange(n)[:, None, None] * W + kj[None, None, :]) >= W)
    s = jnp.einsum('bnqgrd,bnkgd->bngrqk', qb.astype(f32), kb.astype(f32)) * (ATT_HD ** -0.5)
    s = s - alibi_slopes()[:, :, None, None] * dist.astype(f32)
    s = jnp.where(valid[None, :, None, None], s, NEG_INF)
    p = sink_softmax(s, sinks.astype(f32).reshape(ATT_KV_HEADS, ATT_REP)[None, None, :, :, None, None])
    o = jnp.einsum('bngrqk,bnkgd->bnqgrd', p.astype(v.dtype), vb)
    return o.reshape(B, T, ATT_WIDTH)


def swa_buffered(q, k_all, v_all, sinks):
    B, T = q.shape[:2]
    S = k_all.shape[1]
    f32 = jnp.float32
    dist = (S - T + jnp.arange(T))[:, None] - jnp.arange(S)[None, :]
    valid = (dist >= 0) & (dist < WINDOW)
    s = jnp.einsum('btgrd,bsgd->bgrts', q.astype(f32), k_all.astype(f32)) * (ATT_HD ** -0.5)
    s = s - alibi_slopes()[:, :, None, None] * dist.astype(f32)
    s = jnp.where(valid, s, NEG_INF)
    p = sink_softmax(s, sinks.astype(f32).reshape(ATT_KV_HEADS, ATT_REP)[None, :, :, None, None])
    o = jnp.einsum('bgrts,bsgd->btgrd', p.astype(v_all.dtype), v_all)
    return o.reshape(B, T, ATT_WIDTH)


def mix_cd(z, k_buf, v_buf, buf_len, sgu_w, sgu_b, sgu_ln_g, sgu_ln_b, att_sinks):
    zu, zv, zq, zk, zvv = split_cols(z, [D_SGU, D_SGU, ATT_WIDTH, KV_WIDTH, KV_WIDTH])
    yc, v_rows = sgu_mixer(zu, zv, sgu_w, sgu_b, sgu_ln_g, sgu_ln_b)
    B, T, _ = z.shape
    q = zq.reshape(B, T, ATT_KV_HEADS, ATT_REP, ATT_HD)
    k = zk.reshape(B, T, ATT_KV_HEADS, ATT_HD)
    v = zvv.reshape(B, T, ATT_KV_HEADS, ATT_HD)
    if k_buf is None:
        yd = swa_banded(q, k, v, att_sinks)
        k_all, v_all = k, v
    else:
        k_all = jnp.concatenate([k_buf.astype(k.dtype), k], axis=1)
        v_all = jnp.concatenate([v_buf.astype(v.dtype), v], axis=1)
        yd = swa_buffered(q, k_all, v_all, att_sinks)
    return jnp.concatenate([yc, yd], axis=-1), v_rows, k_all[:, -buf_len:], v_all[:, -buf_len:]


def setup_inputs(seed: int = 0) -> dict:
    key = jax.random.key(seed)
    ks = iter(jax.random.split(key, 40))
    d = D_MODEL
    wb = min(WINDOW, PAST_LEN)

    def nrm(shape, s):
        return jax.random.normal(next(ks), shape, jnp.float32) * s

    return {
        'x_prompt': nrm((BATCH, SEQ, d), 1.0),
        'x_sample': nrm((DEC_BATCH, DEC_SEQ, d), 1.0),
        'c_prompt': nrm((BATCH, d), 1.0),
        'c_sample': nrm((DEC_BATCH, d), 1.0),
        'state_conv': nrm((DEC_BATCH, CONV_WIDTH - 1, D_CONV), 0.5),
        'state_gla': nrm((DEC_BATCH, GLA_HEADS, GLA_HK, GLA_HV), 1.0),
        'cache_k': nrm((DEC_BATCH, wb, ATT_KV_HEADS, ATT_HD), 1.0),
        'cache_v': nrm((DEC_BATCH, wb, ATT_KV_HEADS, ATT_HD), 1.0),
        'w_ada': nrm((DEPTH, d, 6 * d), 0.5 * d ** -0.5),
        'b_ada': nrm((DEPTH, 6 * d), 0.02),
        'g_mix': 1.0 + nrm((DEPTH, d), 0.05),
        'g_ffn': 1.0 + nrm((DEPTH, d), 0.05),
        'w_in_ab': nrm((d, IN_AB), d ** -0.5),
        'conv_w': nrm((CONV_WIDTH, D_CONV), CONV_WIDTH ** -0.5),
        'conv_b': nrm((D_CONV,), 0.02),
        'conv_ln_g': 1.0 + nrm((D_CONV,), 0.05),
        'conv_ln_b': nrm((D_CONV,), 0.02),
        'gla_w_gate': nrm((GLA_RANK, GLA_DK), GLA_RANK ** -0.5),
        'gla_b_gate': nrm((GLA_DK,), 0.02),
        'gla_g_norm': 1.0 + nrm((GLA_HV,), 0.05),
        'w_in_cd': nrm((d, IN_CD), d ** -0.5),
        'sgu_w': nrm((SGU_GROUPS, SGU_CHUNK, SGU_CHUNK), SGU_CHUNK ** -0.5),
        'sgu_b': 1.0 + nrm((SGU_GROUPS, SGU_CHUNK), 0.05),
        'sgu_ln_g': 1.0 + nrm((D_SGU,), 0.05),
        'sgu_ln_b': nrm((D_SGU,), 0.02),
        'att_sinks': nrm((ATT_HEADS,), 1.0),
        'w_out': nrm((DEPTH, MIX_WIDTH, d), MIX_WIDTH ** -0.5),
        'w_ffn_gate': nrm((DEPTH, d, D_FF), d ** -0.5),
        'w_ffn_up': nrm((DEPTH, d, D_FF), d ** -0.5),
        'w_ffn_down': nrm((DEPTH, D_FF, d), D_FF ** -0.5),
        'g_final': 1.0 + nrm((d,), 0.05),
    }


def reference(x_prompt, x_sample, c_prompt, c_sample, state_conv, state_gla, cache_k, cache_v,
              w_ada, b_ada, g_mix, g_ffn, w_in_ab, conv_w, conv_b, conv_ln_g, conv_ln_b,
              gla_w_gate, gla_b_gate, gla_g_norm, w_in_cd, sgu_w, sgu_b, sgu_ln_g, sgu_ln_b,
              att_sinks, w_out, w_ffn_gate, w_ffn_up, w_ffn_down, g_final):
    bp = x_prompt.shape[0]
    buf_len = cache_k.shape[1]
    yp, ys = x_prompt, x_sample
    for layer in range(DEPTH):
        mp = adaln(c_prompt, w_ada[layer], b_ada[layer])
        ms = adaln(c_sample, w_ada[layer], b_ada[layer])
        hp = modulate(yp, g_mix[layer], mp[0], mp[1])
        hs = modulate(ys, g_mix[layer], ms[0], ms[1])
        if layer % 2 == 0:
            conv0 = jnp.zeros((bp, CONV_WIDTH - 1, D_CONV), yp.dtype)
            gla0 = jnp.zeros((bp, GLA_HEADS, GLA_HK, GLA_HV), yp.dtype)
            op, st_conv_p, st_gla_p = mix_ab(hp @ w_in_ab, conv0, gla0, conv_w, conv_b, conv_ln_g,
                                             conv_ln_b, gla_w_gate, gla_b_gate, gla_g_norm)
            os_, st_conv_s, st_gla_s = mix_ab(hs @ w_in_ab, state_conv, state_gla, conv_w, conv_b,
                                              conv_ln_g, conv_ln_b, gla_w_gate, gla_b_gate, gla_g_norm)
        else:
            op, _, k_p, v_p = mix_cd(hp @ w_in_cd, None, None, buf_len, sgu_w, sgu_b,
                                     sgu_ln_g, sgu_ln_b, att_sinks)
            os_, sgu_v_s, k_s, v_s = mix_cd(hs @ w_in_cd, cache_k, cache_v, buf_len, sgu_w, sgu_b,
                                            sgu_ln_g, sgu_ln_b, att_sinks)
        yp = yp + mp[2][:, None, :] * (op @ w_out[layer])
        ys = ys + ms[2][:, None, :] * (os_ @ w_out[layer])
        yp = swiglu_sublayer(yp, mp[3], mp[4], mp[5], g_ffn[layer], w_ffn_gate[layer],
                             w_ffn_up[layer], w_ffn_down[layer])
        ys = swiglu_sublayer(ys, ms[3], ms[4], ms[5], g_ffn[layer], w_ffn_gate[layer],
                             w_ffn_up[layer], w_ffn_down[layer])
    y_prompt = rms_norm(yp, g_final)
    y_sample = rms_norm(ys, g_final)
    return (y_prompt, y_sample, st_conv_p, st_conv_s, st_gla_p, st_gla_s, sgu_v_s, k_p, k_s, v_p, v_s)
```

```python
import functools

import jax
import jax.numpy as jnp
from jax import lax
from jax.experimental import pallas as pl
from jax.experimental.pallas import tpu as pltpu

F32 = jnp.float32
BF16 = jnp.bfloat16

D_MODEL = 2048
NORM_EPS = 1e-6
NEG_INF = -1e30
D_CONV = 1024
CONV_WIDTH = 31
CONV_HIST = CONV_WIDTH - 1
GLA_HEADS = 4
GLA_DV = 1024
GLA_DK = 512
GLA_HK = 128
GLA_HV = 256
GLA_RANK = 16
GLA_GATE_NORM = 16.0
D_SGU = 1024
SGU_GROUPS = 4
SGU_GD = 256
ATT_HD = 64
ATT_HEADS = 16
ATT_KV_HEADS = 2
ATT_REP = 8
ATT_WIDTH = 1024
KV_WIDTH = 128
WINDOW = 128
D_FF = 5632
IN_AB = 5136
IN_AB_PAD = 5248
IN_CD = 3328

LANES = 128
SUBLANES = 8
SAMPLE_ROWS = SUBLANES
ROW_BLOCK = 128
VMEM_LIMIT = 56 * 1024 * 1024

ALIBI_SLOPES = tuple(2.0 ** (-8.0 * (h + 1) / ATT_HEADS) for h in range(ATT_HEADS))


def _mm(a, b):
    return jnp.dot(a, b, preferred_element_type=F32)


def _mm_nt(a, b):
    return lax.dot_general(a, b, (((1,), (1,)), ((), ())), preferred_element_type=F32)


def _sigmoid(x):
    return 1.0 / (1.0 + jnp.exp(-x))


def _silu(x):
    return x * _sigmoid(x)


def _gelu_tanh(x):
    c = 0.7978845608028654
    return 0.5 * x * (1.0 + jnp.tanh(c * (x + 0.044715 * (x * x * x))))


def _log_sigmoid(x):
    return jnp.minimum(x, 0.0) - jnp.log(1.0 + jnp.exp(-jnp.abs(x)))


def _rms(x):
    return x * lax.rsqrt(jnp.mean(x * x, axis=-1, keepdims=True) + NORM_EPS)


def _layer_norm(x, g, b):
    mu = jnp.mean(x, axis=-1, keepdims=True)
    xc = x - mu
    return xc * lax.rsqrt(jnp.mean(xc * xc, axis=-1, keepdims=True) + NORM_EPS) * g + b


def _params(*sem):
    return pltpu.CompilerParams(dimension_semantics=sem, vmem_limit_bytes=VMEM_LIMIT)


def _resident(shape):
    nd = len(shape)
    return pl.BlockSpec(shape, lambda *_: (0,) * nd, pipeline_mode=pl.Buffered(1))


def _ada_kernel(c_ref, w_ref, b_ref, o_ref):
    a = _silu(c_ref[...]).astype(BF16)
    o_ref[0] = _mm(a, w_ref[0].astype(BF16)) + b_ref[0]


def _ada(c_all, w_ada, b_ada, tn=512):
    depth, d, n = w_ada.shape
    rows = c_all.shape[0]
    return pl.pallas_call(
        _ada_kernel,
        grid=(depth, n // tn),
        in_specs=[pl.BlockSpec((rows, d), lambda l, j: (0, 0)),
                  pl.BlockSpec((1, d, tn), lambda l, j: (l, 0, j)),
                  pl.BlockSpec((1, 1, tn), lambda l, j: (l, 0, j))],
        out_specs=pl.BlockSpec((1, rows, tn), lambda l, j: (l, 0, j)),
        out_shape=jax.ShapeDtypeStruct((depth, rows, n), F32),
        compiler_params=_params("parallel", "parallel"),
        name="ada",
    )(c_all, w_ada, b_ada.reshape(depth, 1, n))


class _Rows:
    def __init__(self, nseq, T, bb, tT, mod_off):
        assert nseq % bb == 0 and T % tT == 0 and mod_off % bb == 0
        self.nseq, self.T, self.bb, self.tT = nseq, T, bb, tT
        self.grid = (nseq // bb, T // tT)
        self.mod_blk = mod_off // bb

    def act(self, width, col=0):
        return pl.BlockSpec((self.bb, self.tT, width), lambda i, t, *_: (i, t, col))

    def mod(self, k):
        off = self.mod_blk
        return pl.BlockSpec((self.bb, 1, D_MODEL), lambda i, t, *_: (off + i, 0, k))

    def seq(self, *tail):
        nz = (0,) * len(tail)
        return pl.BlockSpec((self.bb,) + tuple(tail), lambda i, t, *_: (i,) + nz)


def _const(shape):
    nd = len(shape)
    return pl.BlockSpec(shape, lambda *_: (0,) * nd)


def _modulated(x, g, scale, shift):
    return (_rms(x) * g) * (1.0 + scale) + shift


def _inproj_kernel(x_ref, sh_ref, sc_ref, g_ref, w_ref, z_ref, *, col_chunk):
    bb, tT, d = x_ref.shape
    h = _modulated(x_ref[...], g_ref[...], sc_ref[...], sh_ref[...])
    h = h.reshape(bb * tT, d).astype(BF16)
    n = w_ref.shape[1]
    for c0 in range(0, n, col_chunk):
        c1 = min(n, c0 + col_chunk)
        z_ref[:, :, c0:c1] = _mm(h, w_ref[:, c0:c1]).reshape(bb, tT, c1 - c0)


def _inproj(rows, x, mods, g, w, name):
    n = w.shape[1]
    return pl.pallas_call(
        functools.partial(_inproj_kernel, col_chunk=512),
        grid=rows.grid,
        in_specs=[rows.act(D_MODEL), rows.mod(0), rows.mod(1), _const((1, 1, D_MODEL)),
                  _resident((D_MODEL, n))],
        out_specs=rows.act(n),
        out_shape=jax.ShapeDtypeStruct((rows.nseq, rows.T, n), F32),
        compiler_params=_params("parallel", "parallel"),
        name=name,
    )(x, mods, mods, g.reshape(1, 1, D_MODEL), w)


def _outproj_kernel(oa_ref, ob_ref, y_ref, gate_ref, w_ref, o_ref):
    bb, tT, d = y_ref.shape
    half = oa_ref.shape[-1]
    oa = oa_ref[...].reshape(bb * tT, half)
    ob = ob_ref[...].reshape(bb * tT, half)
    acc = _mm(oa, w_ref[:half, :]) + _mm(ob, w_ref[half:, :])
    o_ref[...] = y_ref[...] + gate_ref[...] * acc.reshape(bb, tT, d)


def _outproj(rows, oa, ob, y, mods, w, name):
    half = oa.shape[-1]
    return pl.pallas_call(
        _outproj_kernel,
        grid=rows.grid,
        in_specs=[rows.act(half), rows.act(half), rows.act(D_MODEL), rows.mod(2),
                  _resident((D_MODEL, D_MODEL))],
        out_specs=rows.act(D_MODEL),
        out_shape=jax.ShapeDtypeStruct(y.shape, F32),
        compiler_params=_params("parallel", "parallel"),
        name=name,
    )(oa, ob, y, mods, w)


def _ffn_kernel(y_ref, sh_ref, sc_ref, gt_ref, g_ref, wg_ref, wu_ref, wd_ref, gf_ref, o_ref,
                h_scr, acc_scr, *, final):
    f = pl.program_id(2)
    bb, tT, d = y_ref.shape

    @pl.when(f == 0)
    def _():
        h = _modulated(y_ref[...], g_ref[...], sc_ref[...], sh_ref[...])
        h_scr[...] = h.reshape(bb * tT, d).astype(BF16)
        acc_scr[...] = jnp.zeros_like(acc_scr)

    h = h_scr[...]
    a = _mm(h, wg_ref[...])
    b = _mm(h, wu_ref[...])
    p = (_silu(a) * b).astype(BF16)
    acc_scr[...] += _mm(p, wd_ref[...])

    @pl.when(f == pl.num_programs(2) - 1)
    def _():
        out = y_ref[...] + gt_ref[...] * acc_scr[...].reshape(bb, tT, d)
        if final:
            out = _rms(out) * gf_ref[...]
        o_ref[...] = out


def _ffn(rows, y, mods, g, wg, wu, wd, g_final, final, name, tf=512):
    nf = D_FF // tf
    m = rows.bb * rows.tT
    return pl.pallas_call(
        functools.partial(_ffn_kernel, final=final),
        grid=rows.grid + (nf,),
        in_specs=[rows.act(D_MODEL), rows.mod(3), rows.mod(4), rows.mod(5), _const((1, 1, D_MODEL)),
                  pl.BlockSpec((D_MODEL, tf), lambda i, t, f: (0, f)),
                  pl.BlockSpec((D_MODEL, tf), lambda i, t, f: (0, f)),
                  pl.BlockSpec((tf, D_MODEL), lambda i, t, f: (f, 0)),
                  _const((1, 1, D_MODEL))],
        out_specs=rows.act(D_MODEL),
        out_shape=jax.ShapeDtypeStruct(y.shape, F32),
        scratch_shapes=[pltpu.VMEM((m, D_MODEL), BF16), pltpu.VMEM((m, D_MODEL), F32)],
        compiler_params=_params("parallel", "parallel", "arbitrary"),
        name=name,
    )(y, mods, mods, mods, g.reshape(1, 1, D_MODEL), wg, wu, wd, g_final.reshape(1, 1, D_MODEL))


CONV_PAD = 32
CONV_TAIL = 8


def _conv_kernel(za_ref, prev_ref, w_ref, cb_ref, g_ref, b_ref, y_ref, st_ref, ubuf, ybuf,
                 *, t_valid, rc, lc):
    t = pl.program_id(1)
    bb, tT, _ = za_ref.shape
    lo = CONV_PAD - CONV_HIST

    @pl.when(t == 0)
    def _():
        ubuf[:, 0:lo, :] = jnp.zeros((bb, lo, D_CONV), F32)
        ubuf[:, lo:CONV_PAD, :] = prev_ref[...]
        ubuf[:, CONV_PAD + tT:, :] = jnp.zeros((bb, CONV_TAIL, D_CONV), F32)

    @pl.when(t > 0)
    def _():
        ubuf[:, lo:CONV_PAD, :] = ubuf[:, tT + lo:tT + CONV_PAD, :]

    za = za_ref[...]
    ubuf[:, CONV_PAD:CONV_PAD + tT, :] = za[..., :D_CONV] * _sigmoid(za[..., D_CONV:])

    win_rows = rc + CONV_PAD + SUBLANES
    for b in range(bb):
        def row_chunk(i, carry, b=b):
            r0 = pl.multiple_of(i * rc, SUBLANES)
            for c0 in range(0, D_CONV, lc):
                win = ubuf[b, pl.ds(r0, win_rows), c0:c0 + lc]
                acc = jnp.broadcast_to(cb_ref[:, c0:c0 + lc], (rc, lc))
                for s in range(SUBLANES):
                    ws = win[s:s + rc + CONV_PAD, :]
                    for m in range(CONV_PAD // SUBLANES + 1):
                        j = SUBLANES * m + s - lo
                        if 0 <= j < CONV_WIDTH:
                            acc = acc + w_ref[j:j + 1, c0:c0 + lc] * ws[SUBLANES * m:SUBLANES * m + rc, :]
                ybuf[b, pl.ds(r0, rc), c0:c0 + lc] = acc
            return carry
        lax.fori_loop(0, tT // rc, row_chunk, 0)

    y = _layer_norm(ybuf[...], g_ref[...], b_ref[...])
    y_ref[...] = _silu(y).astype(y_ref.dtype)

    @pl.when(t == pl.num_programs(1) - 1)
    def _():
        st_ref[...] = ubuf[:, t_valid + lo:t_valid + CONV_PAD, :]


def _conv(rows, z, prev, conv_w, conv_b, ln_g, ln_b, t_valid, name):
    bb, tT = rows.bb, rows.tT
    rc = min(tT, 64)
    return pl.pallas_call(
        functools.partial(_conv_kernel, t_valid=t_valid, rc=rc, lc=LANES),
        grid=rows.grid,
        in_specs=[rows.act(2 * D_CONV, 0), rows.seq(CONV_HIST, D_CONV),
                  _const((CONV_WIDTH, D_CONV)), _const((1, D_CONV)), _const((1, D_CONV)), _const((1, D_CONV))],
        out_specs=[rows.act(D_CONV), rows.seq(CONV_HIST, D_CONV)],
        out_shape=[jax.ShapeDtypeStruct((rows.nseq, rows.T, D_CONV), BF16),
                   jax.ShapeDtypeStruct((rows.nseq, CONV_HIST, D_CONV), F32)],
        scratch_shapes=[pltpu.VMEM((bb, CONV_PAD + tT + CONV_TAIL, D_CONV), F32),
                        pltpu.VMEM((bb, tT, D_CONV), F32)],
        compiler_params=_params("parallel", "arbitrary"),
        name=name,
    )(z, prev, conv_w, conv_b.reshape(1, -1), ln_g.reshape(1, -1), ln_b.reshape(1, -1))


def _segment_cumsum(x, seg):
    pos = lax.broadcasted_iota(jnp.int32, x.shape, 0) % seg
    s = 1
    while s < seg:
        x = x + jnp.where(pos >= s, pltpu.roll(x, s, 0), 0.0)
        s *= 2
    return x


def _gla_kernel(*refs, nseg, seg, t_valid, has_state):
    if has_state:
        (zq_ref, zk_ref, zv_ref, zg_ref, zr_ref, wg_ref, bg_ref, gn_ref, s0_ref,
         y_ref, st_ref, st_scr) = refs
    else:
        (zq_ref, zk_ref, zv_ref, zg_ref, zr_ref, wg_ref, bg_ref, gn_ref,
         y_ref, st_ref, st_scr) = refs
    t = pl.program_id(1)
    bb, tT, _ = zq_ref.shape
    R = ROW_BLOCK
    assert nseg * seg == R and (bb * tT) % R == 0

    @pl.when(t == 0)
    def _():
        for g in range(nseg):
            for h in range(GLA_HEADS):
                st_scr[g, h] = s0_ref[g, h].T if has_state else jnp.zeros((GLA_HV, GLA_HK), F32)

    def rows_of(ref, c):
        if nseg == 1:
            return ref[0, c * R:(c + 1) * R, :]
        return ref[...].reshape(R, ref.shape[-1])

    row = lax.broadcasted_iota(jnp.int32, (R, R), 0)
    col = lax.broadcasted_iota(jnp.int32, (R, R), 1)
    causal = (row >= col) & ((row // seg) == (col // seg))
    pos = lax.broadcasted_iota(jnp.int32, (R, GLA_DK), 0) % seg
    colseg = lax.broadcasted_iota(jnp.int32, (GLA_HV, R), 1) // seg

    for c in range(bb * tT // R):
        zr = rows_of(zr_ref, c).astype(BF16)
        la = _log_sigmoid(_mm(zr, wg_ref[...]) + bg_ref[...]) * (1.0 / GLA_GATE_NORM)
        q = rows_of(zq_ref, c) * (GLA_HK ** -0.5)
        k = rows_of(zk_ref, c)
        v = rows_of(zv_ref, c)
        zg = rows_of(zg_ref, c)
        if t_valid < seg:
            la = jnp.where(pos < t_valid, la, 0.0)
            k = jnp.where(pos < t_valid, k, 0.0)
        bcum = _segment_cumsum(la, seg)
        b3 = bcum.reshape(nseg, seg, GLA_DK)
        bl3 = b3[:, seg - 1:seg, :]
        k_last = k * jnp.exp(bl3 - b3).reshape(R, GLA_DK)
        dec3 = jnp.exp(bl3)
        q_dec = q * jnp.exp(bcum)
        bmid = b3[:, seg // 2 - 1:seg // 2, :] if nseg == 1 else jnp.zeros_like(bl3)
        q_in = (q * jnp.exp(b3 - bmid).reshape(R, GLA_DK)).astype(BF16)
        k_in = (k * jnp.exp(bmid - b3).reshape(R, GLA_DK)).astype(BF16)
        q_dec = q_dec.astype(BF16)
        k_last = k_last.astype(BF16)
        for h in range(GLA_HEADS):
            ks = slice(h * GLA_HK, (h + 1) * GLA_HK)
            vs = slice(h * GLA_HV, (h + 1) * GLA_HV)
            v_h = v[:, vs]
            att = jnp.where(causal, _mm_nt(q_in[:, ks], k_in[:, ks]), 0.0)
            o = _mm(att.astype(BF16), v_h.astype(BF16))
            vt = v_h.T
            inter = []
            for g in range(nseg):
                st = st_scr[g, h]
                inter.append(_mm_nt(q_dec[g * seg:(g + 1) * seg, ks], st.astype(BF16)))
                vt_g = vt if nseg == 1 else jnp.where(colseg == g, vt, 0.0)
                st_scr[g, h] = st * dec3[g, :, ks] + _mm(vt_g.astype(BF16), k_last[:, ks])
            o = o + (inter[0] if nseg == 1 else jnp.concatenate(inter, axis=0))
            o = _rms(o) * gn_ref[...] * _silu(zg[:, vs])
            o = o.astype(y_ref.dtype)
            if nseg == 1:
                y_ref[0, c * R:(c + 1) * R, vs] = o
            else:
                y_ref[:, :, vs] = o.reshape(bb, tT, GLA_HV)

    @pl.when(t == pl.num_programs(1) - 1)
    def _():
        for g in range(nseg):
            for h in range(GLA_HEADS):
                st_ref[g, h] = st_scr[g, h].T


def _gla(rows, z, s0, w_gate, b_gate, g_norm, t_valid, name):
    bb, tT = rows.bb, rows.tT
    nseg = bb if bb > 1 else 1
    seg = ROW_BLOCK // nseg
    has_state = s0 is not None
    wg = jnp.pad(w_gate, ((0, LANES - GLA_RANK), (0, 0))).astype(BF16)
    state_spec = rows.seq(GLA_HEADS, GLA_HK, GLA_HV)
    in_specs = [rows.act(GLA_DK, 2 * D_CONV // GLA_DK), rows.act(GLA_DK, 2 * D_CONV // GLA_DK + 1),
                rows.act(GLA_DV, (2 * D_CONV + 2 * GLA_DK) // GLA_DV),
                rows.act(GLA_DV, (2 * D_CONV + 2 * GLA_DK) // GLA_DV + 1),
                rows.act(LANES, (IN_AB - GLA_RANK) // LANES),
                _const((LANES, GLA_DK)), _const((1, GLA_DK)), _const((1, GLA_HV))]
    args = [z, z, z, z, z, wg, b_gate.reshape(1, -1), g_norm.reshape(1, -1)]
    if has_state:
        in_specs.append(state_spec)
        args.append(s0)
    return pl.pallas_call(
        functools.partial(_gla_kernel, nseg=nseg, seg=seg, t_valid=t_valid, has_state=has_state),
        grid=rows.grid,
        in_specs=in_specs,
        out_specs=[rows.act(GLA_DV), state_spec],
        out_shape=[jax.ShapeDtypeStruct((rows.nseq, rows.T, GLA_DV), BF16),
                   jax.ShapeDtypeStruct((rows.nseq, GLA_HEADS, GLA_HK, GLA_HV), F32)],
        scratch_shapes=[pltpu.VMEM((nseg, GLA_HEADS, GLA_HV, GLA_HK), F32)],
        compiler_params=_params("parallel", "arbitrary"),
        name=name,
    )(*args)


def _sgu_kernel(zu_ref, zv_ref, w_ref, bias_ref, g_ref, b_ref, y_ref, v_ref):
    bb, tT, _ = zu_ref.shape
    R = ROW_BLOCK
    for c in range(bb * tT // R):
        if bb == 1:
            zu = zu_ref[0, c * R:(c + 1) * R, :]
            zv = zv_ref[0, c * R:(c + 1) * R, :]
        else:
            zu = zu_ref[...].reshape(R, D_SGU)
            zv = zv_ref[...].reshape(R, D_SGU)
        v = _layer_norm(_gelu_tanh(zv), g_ref[...], b_ref[...])
        vb = v.astype(BF16)
        mixed = jnp.concatenate(
            [_mm(w_ref[g], vb[:, g * SGU_GD:(g + 1) * SGU_GD]) for g in range(SGU_GROUPS)], axis=1)
        out = (_gelu_tanh(zu) * (mixed + bias_ref[...])).astype(y_ref.dtype)
        if bb == 1:
            y_ref[0, c * R:(c + 1) * R, :] = out
            v_ref[0, c * R:(c + 1) * R, :] = v
        else:
            y_ref[...] = out.reshape(bb, tT, D_SGU)
            v_ref[...] = v.reshape(bb, tT, D_SGU)


def _sgu(rows, z, w_blocks, bias_rows, ln_g, ln_b, name):
    return pl.pallas_call(
        _sgu_kernel,
        grid=rows.grid,
        in_specs=[rows.act(D_SGU, 0), rows.act(D_SGU, 1),
                  _const((SGU_GROUPS, ROW_BLOCK, ROW_BLOCK)), _const((ROW_BLOCK, D_SGU)),
                  _const((1, D_SGU)), _const((1, D_SGU))],
        out_specs=[rows.act(D_SGU), rows.act(D_SGU)],
        out_shape=[jax.ShapeDtypeStruct((rows.nseq, rows.T, D_SGU), BF16),
                   jax.ShapeDtypeStruct((rows.nseq, rows.T, D_SGU), F32)],
        compiler_params=_params("parallel", "parallel"),
        name=name,
    )(z, z, w_blocks, bias_rows, ln_g.reshape(1, -1), ln_b.reshape(1, -1))


def _swa_kernel(sink_ref, q_ref, kc_ref, vc_ref, kp_ref, vp_ref, y_ref, *, prev_from_grid):
    bb, rq, _ = q_ref.shape
    W = WINDOW
    first_key = jnp.where(pl.program_id(1) > 0, 0, W) if prev_from_grid else 0
    tq = lax.broadcasted_iota(jnp.int32, (rq, 2 * W), 0)
    sk = lax.broadcasted_iota(jnp.int32, (rq, 2 * W), 1)
    dist_i = tq + W - sk
    valid = (dist_i >= 0) & (dist_i < W) & (sk >= first_key)
    dist = dist_i.astype(F32)
    scale = ATT_HD ** -0.5
    for b in range(bb):
        q = q_ref[b]
        kc, vc = kc_ref[b], vc_ref[b]
        if rq < W:
            pad = jnp.zeros((W - rq, KV_WIDTH), F32)
            kc = jnp.concatenate([kc, pad], axis=0)
            vc = jnp.concatenate([vc, pad], axis=0)
        kp, vp = kp_ref[b], vp_ref[b]
        for g in range(ATT_KV_HEADS):
            gs = slice(g * ATT_HD, (g + 1) * ATT_HD)
            qg = jnp.concatenate(
                [q[:, (g * ATT_REP + r) * ATT_HD:(g * ATT_REP + r + 1) * ATT_HD] for r in range(ATT_REP)],
                axis=0).astype(BF16)
            s_all = jnp.concatenate([_mm_nt(qg, kp[:, gs].astype(BF16)),
                                     _mm_nt(qg, kc[:, gs].astype(BF16))], axis=1) * scale
            probs, inv = [], []
            for r in range(ATT_REP):
                h = g * ATT_REP + r
                sink = sink_ref[h]
                s = jnp.where(valid, s_all[r * rq:(r + 1) * rq, :] - ALIBI_SLOPES[h] * dist, NEG_INF)
                m = jnp.maximum(jnp.max(s, axis=-1, keepdims=True), sink)
                p = jnp.exp(s - m)
                inv.append(1.0 / (jnp.sum(p, axis=-1, keepdims=True) + jnp.exp(sink - m)))
                probs.append(p.astype(BF16))
            p_all = jnp.concatenate(probs, axis=0)
            o = _mm(p_all[:, :W], vp[:, gs].astype(BF16)) + _mm(p_all[:, W:], vc[:, gs].astype(BF16))
            for r in range(ATT_REP):
                h = g * ATT_REP + r
                y_ref[b, :, h * ATT_HD:(h + 1) * ATT_HD] = (o[r * rq:(r + 1) * rq, :] * inv[r]).astype(y_ref.dtype)


def _swa(rows, z, sinks, k_prev, v_prev, name):
    bb, tT = rows.bb, rows.tT
    kcol = (2 * D_SGU + ATT_WIDTH) // KV_WIDTH
    from_grid = k_prev is None
    if from_grid:
        assert tT == WINDOW and bb == 1
        kp_spec = pl.BlockSpec((1, WINDOW, KV_WIDTH), lambda i, t: (i, jnp.maximum(t - 1, 0), kcol))
        vp_spec = pl.BlockSpec((1, WINDOW, KV_WIDTH), lambda i, t: (i, jnp.maximum(t - 1, 0), kcol + 1))
        k_prev, v_prev = z, z
    else:
        kp_spec = vp_spec = rows.seq(WINDOW, KV_WIDTH)
    return pl.pallas_call(
        functools.partial(_swa_kernel, prev_from_grid=from_grid),
        grid=rows.grid,
        in_specs=[pl.BlockSpec(memory_space=pltpu.SMEM),
                  rows.act(ATT_WIDTH, 2 * D_SGU // ATT_WIDTH), rows.act(KV_WIDTH, kcol),
                  rows.act(KV_WIDTH, kcol + 1), kp_spec, vp_spec],
        out_specs=rows.act(ATT_WIDTH),
        out_shape=jax.ShapeDtypeStruct((rows.nseq, rows.T, ATT_WIDTH), BF16),
        compiler_params=_params("parallel", "arbitrary"),
        name=name,
    )(sinks, z, z, z, k_prev, v_prev)


def kernel(x_prompt, x_sample, c_prompt, c_sample, state_conv, state_gla, cache_k, cache_v, w_ada, b_ada, g_mix, g_ffn, w_in_ab, conv_w, conv_b, conv_ln_g, conv_ln_b, gla_w_gate, gla_b_gate, gla_g_norm, w_in_cd, sgu_w, sgu_b, sgu_ln_g, sgu_ln_b, att_sinks, w_out, w_ffn_gate, w_ffn_up, w_ffn_down, g_final):
    nb, seq, d = x_prompt.shape
    ndb, dseq, _ = x_sample.shape
    assert d == D_MODEL and dseq <= SAMPLE_ROWS and seq % 512 == 0
    sb = ROW_BLOCK // SAMPLE_ROWS
    assert ndb % sb == 0

    w_ab = jnp.pad(w_in_ab, ((0, 0), (0, IN_AB_PAD - IN_AB))).astype(BF16)
    w_cd = w_in_cd.astype(BF16)
    w_o = w_out.astype(BF16)
    w_fg, w_fu, w_fd = w_ffn_gate.astype(BF16), w_ffn_up.astype(BF16), w_ffn_down.astype(BF16)

    tri = jnp.tril(jnp.ones((ROW_BLOCK, ROW_BLOCK), bool))
    sgu_wp = jnp.where(tri, sgu_w, 0.0).astype(BF16)
    sgu_ws = jnp.where(tri, sgu_w, 0.0)[:, :SAMPLE_ROWS, :SAMPLE_ROWS]
    sgu_ws = jnp.einsum("ab,gts->gatbs", jnp.eye(sb, dtype=F32), sgu_ws).reshape(
        SGU_GROUPS, ROW_BLOCK, ROW_BLOCK).astype(BF16)
    sgu_bias_p = jnp.repeat(sgu_b.T, SGU_GD, axis=1)
    sgu_bias_s = jnp.tile(sgu_bias_p[:SAMPLE_ROWS], (sb, 1))

    xs = jnp.pad(x_sample, ((0, 0), (0, SAMPLE_ROWS - dseq), (0, 0)))
    c_all = jnp.concatenate([c_sample, c_prompt], axis=0)
    mods = _ada(c_all, w_ada, b_ada).reshape(w_ada.shape[0], ndb + nb, 1, 6 * D_MODEL)

    rp_dense = _Rows(nb, seq, 1, 512, ndb)
    rs_dense = _Rows(ndb, SAMPLE_ROWS, 64, SAMPLE_ROWS, 0)
    rp_in = _Rows(nb, seq, 1, 256, ndb)
    rs_in = _Rows(ndb, SAMPLE_ROWS, 32, SAMPLE_ROWS, 0)
    rp_mix = _Rows(nb, seq, 1, 256, ndb)
    rp_att = _Rows(nb, seq, 1, WINDOW, ndb)
    rs_mix = _Rows(ndb, SAMPLE_ROWS, sb, SAMPLE_ROWS, 0)

    yp, ys = x_prompt, xs

    m0 = mods[0]
    zp = _inproj(rp_in, yp, m0, g_mix[0], w_ab, "inproj_ab_p")
    zs = _inproj(rs_in, ys, m0, g_mix[0], w_ab, "inproj_ab_s")
    conv0 = jnp.zeros((nb, CONV_HIST, D_CONV), F32)
    oa_p, st_conv_p = _conv(rp_mix, zp, conv0, conv_w, conv_b, conv_ln_g, conv_ln_b, rp_mix.tT, "conv_p")
    oa_s, st_conv_s = _conv(rs_mix, zs, state_conv, conv_w, conv_b, conv_ln_g, conv_ln_b, dseq, "conv_s")
    ob_p, st_gla_p = _gla(rp_mix, zp, None, gla_w_gate, gla_b_gate, gla_g_norm, ROW_BLOCK, "gla_p")
    ob_s, st_gla_s = _gla(rs_mix, zs, state_gla, gla_w_gate, gla_b_gate, gla_g_norm, dseq, "gla_s")
    yp = _outproj(rp_dense, oa_p, ob_p, yp, m0, w_o[0], "outproj0_p")
    ys = _outproj(rs_dense, oa_s, ob_s, ys, m0, w_o[0], "outproj0_s")
    yp = _ffn(rp_dense, yp, m0, g_ffn[0], w_fg[0], w_fu[0], w_fd[0], g_final, False, "ffn0_p")
    ys = _ffn(rs_dense, ys, m0, g_ffn[0], w_fg[0], w_fu[0], w_fd[0], g_final, False, "ffn0_s")

    m1 = mods[1]
    zp = _inproj(rp_in, yp, m1, g_mix[1], w_cd, "inproj_cd_p")
    zs = _inproj(rs_in, ys, m1, g_mix[1], w_cd, "inproj_cd_s")
    oc_p, _ = _sgu(rp_mix, zp, sgu_wp, sgu_bias_p, sgu_ln_g, sgu_ln_b, "sgu_p")
    oc_s, sgu_v = _sgu(rs_mix, zs, sgu_ws, sgu_bias_s, sgu_ln_g, sgu_ln_b, "sgu_s")
    wb = cache_k.shape[1]
    od_p = _swa(rp_att, zp, att_sinks, None, None, "swa_p")
    od_s = _swa(rs_mix, zs, att_sinks, cache_k.reshape(ndb, wb, KV_WIDTH),
                cache_v.reshape(ndb, wb, KV_WIDTH), "swa_s")
    yp = _outproj(rp_dense, oc_p, od_p, yp, m1, w_o[1], "outproj1_p")
    ys = _outproj(rs_dense, oc_s, od_s, ys, m1, w_o[1], "outproj1_s")
    yp = _ffn(rp_dense, yp, m1, g_ffn[1], w_fg[1], w_fu[1], w_fd[1], g_final, True, "ffn1_p")
    ys = _ffn(rs_dense, ys, m1, g_ffn[1], w_fg[1], w_fu[1], w_fd[1], g_final, True, "ffn1_s")

    k0 = 2 * D_SGU + ATT_WIDTH
    kv_shape = (ATT_KV_HEADS, ATT_HD)
    k_new_p = zp[:, seq - wb:, k0:k0 + KV_WIDTH].reshape(nb, wb, *kv_shape)
    v_new_p = zp[:, seq - wb:, k0 + KV_WIDTH:k0 + 2 * KV_WIDTH].reshape(nb, wb, *kv_shape)
    k_new_s = zs[:, :dseq, k0:k0 + KV_WIDTH].reshape(ndb, dseq, *kv_shape)
    v_new_s = zs[:, :dseq, k0 + KV_WIDTH:k0 + 2 * KV_WIDTH].reshape(ndb, dseq, *kv_shape)
    k_s = jnp.concatenate([cache_k, k_new_s], axis=1)[:, -wb:]
    v_s = jnp.concatenate([cache_v, v_new_s], axis=1)[:, -wb:]
    return (yp, ys[:, :dseq], st_conv_p, st_conv_s, st_gla_p, st_gla_s, sgu_v[:, :dseq],
            k_new_p, k_s, v_new_p, v_s)
```

```python
import functools

import jax
import jax.numpy as jnp
from jax import lax
from jax.experimental import pallas as pl
from jax.experimental.pallas import tpu as pltpu

F32 = jnp.float32
BF16 = jnp.bfloat16

D_MODEL = 2048
NORM_EPS = 1e-6
NEG_INF = -1e30
D_CONV = 1024
CONV_WIDTH = 31
CONV_HIST = CONV_WIDTH - 1
GLA_HEADS = 4
GLA_DV = 1024
GLA_DK = 512
GLA_HK = 128
GLA_HV = 256
GLA_RANK = 16
GLA_GATE_NORM = 16.0
D_SGU = 1024
SGU_GROUPS = 4
SGU_GD = 256
ATT_HD = 64
ATT_HEADS = 16
ATT_KV_HEADS = 2
ATT_REP = 8
ATT_WIDTH = 1024
KV_WIDTH = 128
WINDOW = 128
D_FF = 5632
IN_AB = 5136
IN_AB_PAD = 5248
IN_CD = 3328
N_MODS = 6

LANES = 128
SUBLANES = 8
SAMPLE_ROWS = SUBLANES
ROW_BLOCK = 128
VMEM_LIMIT = 56 * 1024 * 1024

ALIBI_SLOPES = tuple(2.0 ** (-8.0 * (h + 1) / ATT_HEADS) for h in range(ATT_HEADS))


def _mm(a, b):
    return jnp.dot(a, b, preferred_element_type=F32)


def _mm_nt(a, b):
    return lax.dot_general(a, b, (((1,), (1,)), ((), ())), preferred_element_type=F32)


def _sigmoid(x):
    return 1.0 / (1.0 + jnp.exp(-x))


def _silu(x):
    return x * _sigmoid(x)


def _gelu_tanh(x):
    c = 0.7978845608028654
    return 0.5 * x * (1.0 + jnp.tanh(c * (x + 0.044715 * (x * x * x))))


def _log_sigmoid(x):
    return jnp.minimum(x, 0.0) - jnp.log(1.0 + jnp.exp(-jnp.abs(x)))


def _rms(x):
    return x * lax.rsqrt(jnp.mean(x * x, axis=-1, keepdims=True) + NORM_EPS)


def _layer_norm(x, g, b):
    mu = jnp.mean(x, axis=-1, keepdims=True)
    xc = x - mu
    return xc * lax.rsqrt(jnp.mean(xc * xc, axis=-1, keepdims=True) + NORM_EPS) * g + b


def _params(*sem):
    return pltpu.CompilerParams(dimension_semantics=sem, vmem_limit_bytes=VMEM_LIMIT)


def _const(shape):
    nd = len(shape)
    return pl.BlockSpec(shape, lambda *_: (0,) * nd)


def _layer_block(layer, tail, resident=False):
    nz = (0,) * len(tail)
    mode = dict(pipeline_mode=pl.Buffered(1)) if resident else {}
    return pl.BlockSpec((None,) + tuple(tail), lambda *_: (layer,) + nz, **mode)


def _ada_kernel(c_ref, w_ref, b_ref, o_ref, a_scr):
    @pl.when((pl.program_id(0) == 0) & (pl.program_id(1) == 0))
    def _():
        a_scr[...] = _silu(c_ref[...]).astype(BF16)

    o_ref[0] = _mm(a_scr[...], w_ref[0].astype(BF16)) + b_ref[0]


def _ada(c_all, w_ada, b_ada, tn=512):
    depth, d, n = w_ada.shape
    rows = c_all.shape[0]
    return pl.pallas_call(
        _ada_kernel,
        grid=(depth, n // tn),
        in_specs=[pl.BlockSpec((rows, d), lambda l, j: (0, 0)),
                  pl.BlockSpec((1, d, tn), lambda l, j: (l, 0, j)),
                  pl.BlockSpec((1, 1, tn), lambda l, j: (l, 0, j))],
        out_specs=pl.BlockSpec((1, rows, tn), lambda l, j: (l, 0, j)),
        out_shape=jax.ShapeDtypeStruct((depth, rows, n), F32),
        scratch_shapes=[pltpu.VMEM((rows, d), BF16)],
        compiler_params=_params("arbitrary", "arbitrary"),
        name="ada",
    )(c_all, w_ada, b_ada.reshape(depth, 1, n))


class _Rows:
    def __init__(self, nseq, T, bb, tT, mod_off=0, per_row=False):
        assert nseq % bb == 0 and T % tT == 0
        assert (mod_off % tT == 0 and nseq == 1) if per_row else mod_off % bb == 0
        self.nseq, self.T, self.bb, self.tT = nseq, T, bb, tT
        self.grid = (nseq // bb, T // tT)
        self.per_row = per_row
        self.mod_blk = mod_off // (tT if per_row else bb)

    def act(self, width, col=0):
        return pl.BlockSpec((self.bb, self.tT, width), lambda i, t, *_: (i, t, col))

    def mod(self, layer, k):
        off = self.mod_blk
        if self.per_row:
            return pl.BlockSpec((None, 1, self.tT, D_MODEL), lambda i, t, *_: (layer, 0, off + t, k))
        return pl.BlockSpec((None, self.bb, 1, D_MODEL), lambda i, t, *_: (layer, off + i, 0, k))

    def mods_view(self, mods):
        depth, r, n = mods.shape
        return mods.reshape(depth, 1, r, n) if self.per_row else mods.reshape(depth, r, 1, n)

    def seq(self, *tail):
        nz = (0,) * len(tail)
        return pl.BlockSpec((self.bb,) + tuple(tail), lambda i, t, *_: (i,) + nz)


def _modulated(x, g, scale, shift):
    return (_rms(x) * g) * (1.0 + scale) + shift


def _inproj_kernel(x_ref, sh_ref, sc_ref, g_ref, w_ref, z_ref, *, col_chunk):
    bb, tT, d = x_ref.shape
    h = _modulated(x_ref[...], g_ref[...], sc_ref[...], sh_ref[...])
    h = h.reshape(bb * tT, d).astype(BF16)
    n = w_ref.shape[1]
    for c0 in range(0, n, col_chunk):
        c1 = min(n, c0 + col_chunk)
        z_ref[:, :, c0:c1] = _mm(h, w_ref[:, c0:c1]).reshape(bb, tT, c1 - c0)


def _inproj(rows, x, mods, layer, g, w, name):
    n = w.shape[1]
    mv = rows.mods_view(mods)
    return pl.pallas_call(
        functools.partial(_inproj_kernel, col_chunk=512),
        grid=rows.grid,
        in_specs=[rows.act(D_MODEL), rows.mod(layer, 0), rows.mod(layer, 1),
                  _layer_block(layer, (1, 1, D_MODEL)),
                  pl.BlockSpec((D_MODEL, n), lambda *_: (0, 0), pipeline_mode=pl.Buffered(1))],
        out_specs=rows.act(n),
        out_shape=jax.ShapeDtypeStruct((rows.nseq, rows.T, n), F32),
        compiler_params=_params("parallel", "parallel"),
        name=name,
    )(x, mv, mv, g.reshape(-1, 1, 1, D_MODEL), w)


def _outproj_kernel(oa_ref, ob_ref, y_ref, gate_ref, w_ref, o_ref):
    bb, tT, d = y_ref.shape
    half = oa_ref.shape[-1]
    oa = oa_ref[...].reshape(bb * tT, half)
    ob = ob_ref[...].reshape(bb * tT, half)
    acc = _mm(oa, w_ref[:half, :]) + _mm(ob, w_ref[half:, :])
    o_ref[...] = y_ref[...] + gate_ref[...] * acc.reshape(bb, tT, d)


def _outproj(rows, oa, ob, y, mods, layer, w, name):
    half = oa.shape[-1]
    return pl.pallas_call(
        _outproj_kernel,
        grid=rows.grid,
        in_specs=[rows.act(half), rows.act(half), rows.act(D_MODEL), rows.mod(layer, 2),
                  _layer_block(layer, (D_MODEL, D_MODEL), resident=True)],
        out_specs=rows.act(D_MODEL),
        out_shape=jax.ShapeDtypeStruct(y.shape, F32),
        compiler_params=_params("parallel", "parallel"),
        name=name,
    )(oa, ob, y, rows.mods_view(mods), w)


def _ffn_kernel(y_ref, sh_ref, sc_ref, gt_ref, g_ref, wg_ref, wu_ref, wd_ref, gf_ref, o_ref,
                h_scr, acc_scr, *, final):
    f = pl.program_id(2)
    bb, tT, d = y_ref.shape

    @pl.when(f == 0)
    def _():
        h = _modulated(y_ref[...], g_ref[...], sc_ref[...], sh_ref[...])
        h_scr[...] = h.reshape(bb * tT, d).astype(BF16)
        acc_scr[...] = jnp.zeros_like(acc_scr)

    h = h_scr[...]
    a = _mm(h, wg_ref[...])
    b = _mm(h, wu_ref[...])
    p = (_silu(a) * b).astype(BF16)
    acc_scr[...] += _mm(p, wd_ref[...])

    @pl.when(f == pl.num_programs(2) - 1)
    def _():
        out = y_ref[...] + gt_ref[...] * acc_scr[...].reshape(bb, tT, d)
        if final:
            out = _rms(out) * gf_ref[...]
        o_ref[...] = out


def _ffn(rows, y, mods, layer, g, wg, wu, wd, g_final, final, name, tf=512):
    nf = D_FF // tf
    m = rows.bb * rows.tT
    mv = rows.mods_view(mods)
    return pl.pallas_call(
        functools.partial(_ffn_kernel, final=final),
        grid=rows.grid + (nf,),
        in_specs=[rows.act(D_MODEL), rows.mod(layer, 3), rows.mod(layer, 4), rows.mod(layer, 5),
                  _layer_block(layer, (1, 1, D_MODEL)),
                  pl.BlockSpec((None, D_MODEL, tf), lambda i, t, f: (layer, 0, f)),
                  pl.BlockSpec((None, D_MODEL, tf), lambda i, t, f: (layer, 0, f)),
                  pl.BlockSpec((None, tf, D_MODEL), lambda i, t, f: (layer, f, 0)),
                  _const((1, 1, D_MODEL))],
        out_specs=rows.act(D_MODEL),
        out_shape=jax.ShapeDtypeStruct(y.shape, F32),
        scratch_shapes=[pltpu.VMEM((m, D_MODEL), BF16), pltpu.VMEM((m, D_MODEL), F32)],
        compiler_params=_params("parallel", "parallel", "arbitrary"),
        name=name,
    )(y, mv, mv, mv, g.reshape(-1, 1, 1, D_MODEL), wg, wu, wd, g_final.reshape(1, 1, D_MODEL))


CONV_PAD = 32


def _conv_kernel(za_ref, prev_ref, w_ref, cb_ref, g_ref, b_ref, y_ref, st_ref, ubuf, ybuf,
                 *, t_valid, rc, lc):
    t = pl.program_id(1)
    bb, tT, _ = za_ref.shape
    lo = CONV_PAD - CONV_HIST
    assert lc == LANES
    nlc = D_CONV // lc

    @pl.when(t == 0)
    def _():
        for b in range(bb):
            for ci in range(nlc):
                ubuf[b * nlc + ci, lo:CONV_PAD, :] = prev_ref[b, :, ci * lc:(ci + 1) * lc]

    @pl.when(t > 0)
    def _():
        ubuf[:, lo:CONV_PAD, :] = ubuf[:, tT + lo:tT + CONV_PAD, :]

    for b in range(bb):
        for ci in range(nlc):
            c0 = ci * lc
            ubuf[b * nlc + ci, CONV_PAD:CONV_PAD + tT, :] = (
                za_ref[b, :, c0:c0 + lc] * _sigmoid(za_ref[b, :, D_CONV + c0:D_CONV + c0 + lc]))

    for b in range(bb):
        def row_chunk(i, carry, b=b):
            r0 = pl.multiple_of(i * rc, SUBLANES)
            for ci in range(nlc):
                c0 = ci * lc
                acc = jnp.broadcast_to(cb_ref[:, c0:c0 + lc], (rc, lc))
                for j in range(CONV_WIDTH):
                    acc = acc + w_ref[j:j + 1, c0:c0 + lc] * ubuf[b * nlc + ci, pl.ds(r0 + lo + j, rc), :]
                ybuf[b, pl.ds(r0, rc), c0:c0 + lc] = acc
            return carry
        lax.fori_loop(0, tT // rc, row_chunk, 0)

    y = _layer_norm(ybuf[...], g_ref[...], b_ref[...])
    y_ref[...] = _silu(y).astype(y_ref.dtype)

    @pl.when(t == pl.num_programs(1) - 1)
    def _():
        for b in range(bb):
            for ci in range(nlc):
                st_ref[b, :, ci * lc:(ci + 1) * lc] = ubuf[b * nlc + ci, t_valid + lo:t_valid + CONV_PAD, :]


def _conv(rows, z, prev, conv_w, conv_b, ln_g, ln_b, t_valid, name):
    bb, tT = rows.bb, rows.tT
    rc = min(tT, 64)
    return pl.pallas_call(
        functools.partial(_conv_kernel, t_valid=t_valid, rc=rc, lc=LANES),
        grid=rows.grid,
        in_specs=[rows.act(2 * D_CONV, 0), rows.seq(CONV_HIST, D_CONV),
                  _const((CONV_WIDTH, D_CONV)), _const((1, D_CONV)), _const((1, D_CONV)), _const((1, D_CONV))],
        out_specs=[rows.act(D_CONV), rows.seq(CONV_HIST, D_CONV)],
        out_shape=[jax.ShapeDtypeStruct((rows.nseq, rows.T, D_CONV), BF16),
                   jax.ShapeDtypeStruct((rows.nseq, CONV_HIST, D_CONV), F32)],
        scratch_shapes=[pltpu.VMEM((bb * (D_CONV // LANES), CONV_PAD + tT, LANES), F32),
                        pltpu.VMEM((bb, tT, D_CONV), F32)],
        compiler_params=_params("parallel", "arbitrary"),
        name=name,
    )(z, prev, conv_w, conv_b.reshape(1, -1), ln_g.reshape(1, -1), ln_b.reshape(1, -1))


def _segment_cumsum(x, seg):
    pos = lax.broadcasted_iota(jnp.int32, x.shape, 0) % seg
    s = 1
    while s < seg:
        x = x + jnp.where(pos >= s, pltpu.roll(x, s, 0), 0.0)
        s *= 2
    return x


def _gla_kernel(*refs, nseg, seg, t_valid, has_state):
    if has_state:
        (zq_ref, zk_ref, zv_ref, zg_ref, zr_ref, wg_ref, bg_ref, gn_ref, s0_ref,
         y_ref, st_ref, st_scr) = refs
    else:
        (zq_ref, zk_ref, zv_ref, zg_ref, zr_ref, wg_ref, bg_ref, gn_ref,
         y_ref, st_ref, st_scr) = refs
    t = pl.program_id(1)
    bb, tT, _ = zq_ref.shape
    R = ROW_BLOCK
    assert nseg * seg == R and (bb * tT) % R == 0

    @pl.when(t == 0)
    def _():
        for g in range(nseg):
            for h in range(GLA_HEADS):
                st_scr[g, h] = s0_ref[g, h].T if has_state else jnp.zeros((GLA_HV, GLA_HK), F32)

    def rows_of(ref, c):
        if nseg == 1:
            return ref[0, c * R:(c + 1) * R, :]
        return ref[...].reshape(R, ref.shape[-1])

    row = lax.broadcasted_iota(jnp.int32, (R, R), 0)
    col = lax.broadcasted_iota(jnp.int32, (R, R), 1)
    causal = (row >= col) & ((row // seg) == (col // seg))
    pos = lax.broadcasted_iota(jnp.int32, (R, GLA_DK), 0) % seg
    colseg = lax.broadcasted_iota(jnp.int32, (GLA_HV, R), 1) // seg

    for c in range(bb * tT // R):
        zr = rows_of(zr_ref, c).astype(BF16)
        la = _log_sigmoid(_mm(zr, wg_ref[...]) + bg_ref[...]) * (1.0 / GLA_GATE_NORM)
        q = rows_of(zq_ref, c) * (GLA_HK ** -0.5)
        k = rows_of(zk_ref, c)
        v = rows_of(zv_ref, c)
        zg = rows_of(zg_ref, c)
        if t_valid < seg:
            la = jnp.where(pos < t_valid, la, 0.0)
            k = jnp.where(pos < t_valid, k, 0.0)
        bcum = _segment_cumsum(la, seg)
        b3 = bcum.reshape(nseg, seg, GLA_DK)
        bl3 = b3[:, seg - 1:seg, :]
        k_last = k * jnp.exp(bl3 - b3).reshape(R, GLA_DK)
        dec3 = jnp.exp(bl3)
        q_dec = q * jnp.exp(bcum)
        bmid = b3[:, seg // 2 - 1:seg // 2, :] if nseg == 1 else jnp.zeros_like(bl3)
        q_in = (q * jnp.exp(b3 - bmid).reshape(R, GLA_DK)).astype(BF16)
        k_in = (k * jnp.exp(bmid - b3).reshape(R, GLA_DK)).astype(BF16)
        q_dec = q_dec.astype(BF16)
        k_last = k_last.astype(BF16)
        for h in range(GLA_HEADS):
            ks = slice(h * GLA_HK, (h + 1) * GLA_HK)
            vs = slice(h * GLA_HV, (h + 1) * GLA_HV)
            v_h = v[:, vs]
            att = jnp.where(causal, _mm_nt(q_in[:, ks], k_in[:, ks]), 0.0)
            o = _mm(att.astype(BF16), v_h.astype(BF16))
            vt = v_h.T
            inter = []
            for g in range(nseg):
                st = st_scr[g, h]
                inter.append(_mm_nt(q_dec[g * seg:(g + 1) * seg, ks], st.astype(BF16)))
                vt_g = vt if nseg == 1 else jnp.where(colseg == g, vt, 0.0)
                st_scr[g, h] = st * dec3[g, :, ks] + _mm(vt_g.astype(BF16), k_last[:, ks])
            o = o + (inter[0] if nseg == 1 else jnp.concatenate(inter, axis=0))
            o = _rms(o) * gn_ref[...] * _silu(zg[:, vs])
            o = o.astype(y_ref.dtype)
            if nseg == 1:
                y_ref[0, c * R:(c + 1) * R, vs] = o
            else:
                y_ref[:, :, vs] = o.reshape(bb, tT, GLA_HV)

    @pl.when(t == pl.num_programs(1) - 1)
    def _():
        for g in range(nseg):
            for h in range(GLA_HEADS):
                st_ref[g, h] = st_scr[g, h].T


def _gla(rows, z, s0, w_gate, b_gate, g_norm, t_valid, name):
    bb = rows.bb
    nseg = bb if bb > 1 else 1
    seg = ROW_BLOCK // nseg
    has_state = s0 is not None
    wg = jnp.pad(w_gate, ((0, LANES - GLA_RANK), (0, 0))).astype(BF16)
    state_spec = rows.seq(GLA_HEADS, GLA_HK, GLA_HV)
    in_specs = [rows.act(GLA_DK, 2 * D_CONV // GLA_DK), rows.act(GLA_DK, 2 * D_CONV // GLA_DK + 1),
                rows.act(GLA_DV, (2 * D_CONV + 2 * GLA_DK) // GLA_DV),
                rows.act(GLA_DV, (2 * D_CONV + 2 * GLA_DK) // GLA_DV + 1),
                rows.act(LANES, (IN_AB - GLA_RANK) // LANES),
                _const((LANES, GLA_DK)), _const((1, GLA_DK)), _const((1, GLA_HV))]
    args = [z, z, z, z, z, wg, b_gate.reshape(1, -1), g_norm.reshape(1, -1)]
    if has_state:
        in_specs.append(state_spec)
        args.append(s0)
    return pl.pallas_call(
        functools.partial(_gla_kernel, nseg=nseg, seg=seg, t_valid=t_valid, has_state=has_state),
        grid=rows.grid,
        in_specs=in_specs,
        out_specs=[rows.act(GLA_DV), state_spec],
        out_shape=[jax.ShapeDtypeStruct((rows.nseq, rows.T, GLA_DV), BF16),
                   jax.ShapeDtypeStruct((rows.nseq, GLA_HEADS, GLA_HK, GLA_HV), F32)],
        scratch_shapes=[pltpu.VMEM((nseg, GLA_HEADS, GLA_HV, GLA_HK), F32)],
        compiler_params=_params("parallel", "arbitrary"),
        name=name,
    )(*args)


def _sgu_kernel(zu_ref, zv_ref, w_ref, bias_ref, g_ref, b_ref, y_ref, *maybe_v_ref):
    bb, tT, _ = zu_ref.shape
    R = ROW_BLOCK
    for c in range(bb * tT // R):
        if bb == 1:
            zu = zu_ref[0, c * R:(c + 1) * R, :]
            zv = zv_ref[0, c * R:(c + 1) * R, :]
        else:
            zu = zu_ref[...].reshape(R, D_SGU)
            zv = zv_ref[...].reshape(R, D_SGU)
        v = _layer_norm(_gelu_tanh(zv), g_ref[...], b_ref[...])
        vb = v.astype(BF16)
        mixed = jnp.concatenate(
            [_mm(w_ref[g], vb[:, g * SGU_GD:(g + 1) * SGU_GD]) for g in range(SGU_GROUPS)], axis=1)
        out = (_gelu_tanh(zu) * (mixed + bias_ref[...])).astype(y_ref.dtype)
        if bb == 1:
            y_ref[0, c * R:(c + 1) * R, :] = out
        else:
            y_ref[...] = out.reshape(bb, tT, D_SGU)
        for v_ref in maybe_v_ref:
            if bb == 1:
                v_ref[0, c * R:(c + 1) * R, :] = v
            else:
                v_ref[...] = v.reshape(bb, tT, D_SGU)


def _sgu(rows, z, w_blocks, bias_rows, ln_g, ln_b, want_v, name):
    out_specs = [rows.act(D_SGU)]
    out_shape = [jax.ShapeDtypeStruct((rows.nseq, rows.T, D_SGU), BF16)]
    if want_v:
        out_specs.append(rows.act(D_SGU))
        out_shape.append(jax.ShapeDtypeStruct((rows.nseq, rows.T, D_SGU), F32))
    return pl.pallas_call(
        _sgu_kernel,
        grid=rows.grid,
        in_specs=[rows.act(D_SGU, 0), rows.act(D_SGU, 1),
                  _const((SGU_GROUPS, ROW_BLOCK, ROW_BLOCK)), _const((ROW_BLOCK, D_SGU)),
                  _const((1, D_SGU)), _const((1, D_SGU))],
        out_specs=out_specs,
        out_shape=out_shape,
        compiler_params=_params("parallel", "parallel"),
        name=name,
    )(z, z, w_blocks, bias_rows, ln_g.reshape(1, -1), ln_b.reshape(1, -1))


def _swa_kernel(sink_ref, q_ref, kc_ref, vc_ref, kp_ref, vp_ref, y_ref, *, prev_from_grid):
    bb, rq, _ = q_ref.shape
    W = WINDOW
    first_key = jnp.where(pl.program_id(1) > 0, 0, W) if prev_from_grid else 0
    tq = lax.broadcasted_iota(jnp.int32, (rq, 2 * W), 0)
    sk = lax.broadcasted_iota(jnp.int32, (rq, 2 * W), 1)
    dist_i = tq + W - sk
    valid = (dist_i >= 0) & (dist_i < W) & (sk >= first_key)
    dist = dist_i.astype(F32)
    scale = ATT_HD ** -0.5
    for b in range(bb):
        q = q_ref[b]
        kc, vc = kc_ref[b], vc_ref[b]
        if rq < W:
            pad = jnp.zeros((W - rq, KV_WIDTH), F32)
            kc = jnp.concatenate([kc, pad], axis=0)
            vc = jnp.concatenate([vc, pad], axis=0)
        kp, vp = kp_ref[b], vp_ref[b]
        for g in range(ATT_KV_HEADS):
            gs = slice(g * ATT_HD, (g + 1) * ATT_HD)
            qg = jnp.concatenate(
                [q[:, (g * ATT_REP + r) * ATT_HD:(g * ATT_REP + r + 1) * ATT_HD] for r in range(ATT_REP)],
                axis=0).astype(BF16)
            s_all = jnp.concatenate([_mm_nt(qg, kp[:, gs].astype(BF16)),
                                     _mm_nt(qg, kc[:, gs].astype(BF16))], axis=1) * scale
            probs, inv = [], []
            for r in range(ATT_REP):
                h = g * ATT_REP + r
                sink = sink_ref[h]
                s = jnp.where(valid, s_all[r * rq:(r + 1) * rq, :] - ALIBI_SLOPES[h] * dist, NEG_INF)
                m = jnp.maximum(jnp.max(s, axis=-1, keepdims=True), sink)
                p = jnp.exp(s - m)
                inv.append(1.0 / (jnp.sum(p, axis=-1, keepdims=True) + jnp.exp(sink - m)))
                probs.append(p.astype(BF16))
            p_all = jnp.concatenate(probs, axis=0)
            o = _mm(p_all[:, :W], vp[:, gs].astype(BF16)) + _mm(p_all[:, W:], vc[:, gs].astype(BF16))
            for r in range(ATT_REP):
                h = g * ATT_REP + r
                y_ref[b, :, h * ATT_HD:(h + 1) * ATT_HD] = (o[r * rq:(r + 1) * rq, :] * inv[r]).astype(y_ref.dtype)


def _swa(rows, z, sinks, k_prev, v_prev, name):
    bb, tT = rows.bb, rows.tT
    kcol = (2 * D_SGU + ATT_WIDTH) // KV_WIDTH
    from_grid = k_prev is None
    if from_grid:
        assert tT == WINDOW and bb == 1
        kp_spec = pl.BlockSpec((1, WINDOW, KV_WIDTH), lambda i, t: (i, jnp.maximum(t - 1, 0), kcol))
        vp_spec = pl.BlockSpec((1, WINDOW, KV_WIDTH), lambda i, t: (i, jnp.maximum(t - 1, 0), kcol + 1))
        k_prev, v_prev = z, z
    else:
        kp_spec = vp_spec = rows.seq(WINDOW, KV_WIDTH)
    return pl.pallas_call(
        functools.partial(_swa_kernel, prev_from_grid=from_grid),
        grid=rows.grid,
        in_specs=[pl.BlockSpec(memory_space=pltpu.SMEM),
                  rows.act(ATT_WIDTH, 2 * D_SGU // ATT_WIDTH), rows.act(KV_WIDTH, kcol),
                  rows.act(KV_WIDTH, kcol + 1), kp_spec, vp_spec],
        out_specs=rows.act(ATT_WIDTH),
        out_shape=jax.ShapeDtypeStruct((rows.nseq, rows.T, ATT_WIDTH), BF16),
        compiler_params=_params("parallel", "arbitrary"),
        name=name,
    )(sinks, z, z, z, k_prev, v_prev)


def kernel(x_prompt, x_sample, c_prompt, c_sample, state_conv, state_gla, cache_k, cache_v, w_ada, b_ada, g_mix, g_ffn, w_in_ab, conv_w, conv_b, conv_ln_g, conv_ln_b, gla_w_gate, gla_b_gate, gla_g_norm, w_in_cd, sgu_w, sgu_b, sgu_ln_g, sgu_ln_b, att_sinks, w_out, w_ffn_gate, w_ffn_up, w_ffn_down, g_final):
    nb, seq, d = x_prompt.shape
    ndb, dseq, _ = x_sample.shape
    ns = ndb * dseq
    assert d == D_MODEL and dseq <= SAMPLE_ROWS and seq % 512 == 0 and ns % 256 == 0
    sb = ROW_BLOCK // SAMPLE_ROWS
    assert ndb % sb == 0

    w_ab = jnp.pad(w_in_ab, ((0, 0), (0, IN_AB_PAD - IN_AB))).astype(BF16)
    w_cd = w_in_cd.astype(BF16)
    w_o = w_out.astype(BF16)
    w_fg, w_fu, w_fd = w_ffn_gate.astype(BF16), w_ffn_up.astype(BF16), w_ffn_down.astype(BF16)

    tri = jnp.tril(jnp.ones((ROW_BLOCK, ROW_BLOCK), bool))
    sgu_wp = jnp.where(tri, sgu_w, 0.0).astype(BF16)
    sgu_ws = jnp.where(tri, sgu_w, 0.0)[:, :SAMPLE_ROWS, :SAMPLE_ROWS]
    sgu_ws = jnp.einsum("ab,gts->gatbs", jnp.eye(sb, dtype=F32), sgu_ws).reshape(
        SGU_GROUPS, ROW_BLOCK, ROW_BLOCK).astype(BF16)
    sgu_bias_p = jnp.repeat(sgu_b.T, SGU_GD, axis=1)
    sgu_bias_s = jnp.tile(sgu_bias_p[:SAMPLE_ROWS], (sb, 1))

    c_all = jnp.concatenate([jnp.repeat(c_sample, dseq, axis=0), c_prompt], axis=0)
    mods = _ada(c_all, w_ada, b_ada)

    rp_dense = _Rows(nb, seq, 1, 512, mod_off=ns)
    rs_dense = _Rows(1, ns, 1, min(ns, 512), per_row=True)
    rp_in = _Rows(nb, seq, 1, 256, mod_off=ns)
    rs_in = _Rows(1, ns, 1, 256, per_row=True)
    rp_mix = _Rows(nb, seq, 1, 256)
    rp_att = _Rows(nb, seq, 1, WINDOW)
    rs_mix = _Rows(ndb, SAMPLE_ROWS, sb, SAMPLE_ROWS)

    def to_mixer(z):
        z = z.reshape(ndb, dseq, z.shape[-1])
        return jnp.pad(z, ((0, 0), (0, SAMPLE_ROWS - dseq), (0, 0)))

    def to_dense(o):
        return o[:, :dseq].reshape(1, ns, o.shape[-1])

    yp, ys = x_prompt, x_sample.reshape(1, ns, d)

    zp = _inproj(rp_in, yp, mods, 0, g_mix, w_ab, "inproj_ab_p")
    zs = to_mixer(_inproj(rs_in, ys, mods, 0, g_mix, w_ab, "inproj_ab_s"))
    conv0 = jnp.zeros((nb, CONV_HIST, D_CONV), F32)
    oa_p, st_conv_p = _conv(rp_mix, zp, conv0, conv_w, conv_b, conv_ln_g, conv_ln_b, rp_mix.tT, "conv_p")
    oa_s, st_conv_s = _conv(rs_mix, zs, state_conv, conv_w, conv_b, conv_ln_g, conv_ln_b, dseq, "conv_s")
    ob_p, st_gla_p = _gla(rp_mix, zp, None, gla_w_gate, gla_b_gate, gla_g_norm, ROW_BLOCK, "gla_p")
    ob_s, st_gla_s = _gla(rs_mix, zs, state_gla, gla_w_gate, gla_b_gate, gla_g_norm, dseq, "gla_s")
    yp = _outproj(rp_dense, oa_p, ob_p, yp, mods, 0, w_o, "outproj0_p")
    ys = _outproj(rs_dense, to_dense(oa_s), to_dense(ob_s), ys, mods, 0, w_o, "outproj0_s")
    yp = _ffn(rp_dense, yp, mods, 0, g_ffn, w_fg, w_fu, w_fd, g_final, False, "ffn0_p")
    ys = _ffn(rs_dense, ys, mods, 0, g_ffn, w_fg, w_fu, w_fd, g_final, False, "ffn0_s")

    zp = _inproj(rp_in, yp, mods, 1, g_mix, w_cd, "inproj_cd_p")
    zs = to_mixer(_inproj(rs_in, ys, mods, 1, g_mix, w_cd, "inproj_cd_s"))
    (oc_p,) = _sgu(rp_mix, zp, sgu_wp, sgu_bias_p, sgu_ln_g, sgu_ln_b, False, "sgu_p")
    oc_s, sgu_v = _sgu(rs_mix, zs, sgu_ws, sgu_bias_s, sgu_ln_g, sgu_ln_b, True, "sgu_s")
    wb = cache_k.shape[1]
    od_p = _swa(rp_att, zp, att_sinks, None, None, "swa_p")
    od_s = _swa(rs_mix, zs, att_sinks, cache_k.reshape(ndb, wb, KV_WIDTH),
                cache_v.reshape(ndb, wb, KV_WIDTH), "swa_s")
    yp = _outproj(rp_dense, oc_p, od_p, yp, mods, 1, w_o, "outproj1_p")
    ys = _outproj(rs_dense, to_dense(oc_s), to_dense(od_s), ys, mods, 1, w_o, "outproj1_s")
    yp = _ffn(rp_dense, yp, mods, 1, g_ffn, w_fg, w_fu, w_fd, g_final, True, "ffn1_p")
    ys = _ffn(rs_dense, ys, mods, 1, g_ffn, w_fg, w_fu, w_fd, g_final, True, "ffn1_s")

    k0 = 2 * D_SGU + ATT_WIDTH
    kv_shape = (ATT_KV_HEADS, ATT_HD)
    k_new_p = zp[:, seq - wb:, k0:k0 + KV_WIDTH].reshape(nb, wb, *kv_shape)
    v_new_p = zp[:, seq - wb:, k0 + KV_WIDTH:k0 + 2 * KV_WIDTH].reshape(nb, wb, *kv_shape)
    k_new_s = zs[:, :dseq, k0:k0 + KV_WIDTH].reshape(ndb, dseq, *kv_shape)
    v_new_s = zs[:, :dseq, k0 + KV_WIDTH:k0 + 2 * KV_WIDTH].reshape(ndb, dseq, *kv_shape)
    k_s = jnp.concatenate([cache_k, k_new_s], axis=1)[:, -wb:]
    v_s = jnp.concatenate([cache_v, v_new_s], axis=1)[:, -wb:]
    return (yp, ys.reshape(ndb, dseq, d), st_conv_p, st_conv_s, st_gla_p, st_gla_s, sgu_v[:, :dseq],
            k_new_p, k_s, v_new_p, v_s)
```

```python
import functools

import jax
import jax.numpy as jnp
from jax import lax
from jax.experimental import pallas as pl
from jax.experimental.pallas import tpu as pltpu

F32 = jnp.float32
BF16 = jnp.bfloat16

D_MODEL = 2048
NORM_EPS = 1e-6
NEG_INF = -1e30
D_CONV = 1024
CONV_WIDTH = 31
CONV_HIST = CONV_WIDTH - 1
GLA_HEADS = 4
GLA_DV = 1024
GLA_DK = 512
GLA_HK = 128
GLA_HV = 256
GLA_RANK = 16
GLA_GATE_NORM = 16.0
D_SGU = 1024
SGU_GROUPS = 4
SGU_GD = 256
ATT_HD = 64
ATT_HEADS = 16
ATT_KV_HEADS = 2
ATT_REP = 8
ATT_WIDTH = 1024
KV_WIDTH = 128
WINDOW = 128
D_FF = 5632
IN_AB = 5136
IN_AB_PAD = 5248
IN_CD = 3328
N_MODS = 6

LANES = 128
SUBLANES = 8
SAMPLE_ROWS = SUBLANES
ROW_BLOCK = 128
VMEM_LIMIT = 56 * 1024 * 1024

ALIBI_SLOPES = tuple(2.0 ** (-8.0 * (h + 1) / ATT_HEADS) for h in range(ATT_HEADS))


def _mm(a, b):
    return jnp.dot(a, b, preferred_element_type=F32)


def _mm_nt(a, b):
    return lax.dot_general(a, b, (((1,), (1,)), ((), ())), preferred_element_type=F32)


def _sigmoid(x):
    return 1.0 / (1.0 + jnp.exp(-x))


def _silu(x):
    return x * _sigmoid(x)


def _gelu_tanh(x):
    c = 0.7978845608028654
    return 0.5 * x * (1.0 + jnp.tanh(c * (x + 0.044715 * (x * x * x))))


def _log_sigmoid(x):
    return jnp.minimum(x, 0.0) - jnp.log(1.0 + jnp.exp(-jnp.abs(x)))


def _rms(x):
    return x * lax.rsqrt(jnp.mean(x * x, axis=-1, keepdims=True) + NORM_EPS)


def _layer_norm(x, g, b):
    mu = jnp.mean(x, axis=-1, keepdims=True)
    xc = x - mu
    return xc * lax.rsqrt(jnp.mean(xc * xc, axis=-1, keepdims=True) + NORM_EPS) * g + b


def _params(*sem):
    return pltpu.CompilerParams(dimension_semantics=sem, vmem_limit_bytes=VMEM_LIMIT)


def _const(shape):
    nd = len(shape)
    return pl.BlockSpec(shape, lambda *_: (0,) * nd)


def _layer_block(layer, tail, resident=False):
    nz = (0,) * len(tail)
    mode = dict(pipeline_mode=pl.Buffered(1)) if resident else {}
    return pl.BlockSpec((None,) + tuple(tail), lambda *_: (layer,) + nz, **mode)


def _ada_kernel(c_ref, w_ref, b_ref, rows_ref, seq_ref, a_scr):
    @pl.when((pl.program_id(0) == 0) & (pl.program_id(1) == 0))
    def _():
        a_scr[...] = _silu(c_ref[...]).astype(BF16)

    res = _mm(a_scr[...], w_ref[0].astype(BF16)) + b_ref[0]
    ns = rows_ref.shape[1]
    rows_ref[0] = res[:ns]
    for r in range(seq_ref.shape[1]):
        seq_ref[0, r] = res[ns + r:ns + r + 1]


def _ada(c_all, n_rows, w_ada, b_ada, tn=1024):
    depth, d, n = w_ada.shape
    rows = c_all.shape[0]
    nb = rows - n_rows
    return pl.pallas_call(
        _ada_kernel,
        grid=(depth, n // tn),
        in_specs=[pl.BlockSpec((rows, d), lambda l, j: (0, 0)),
                  pl.BlockSpec((1, d, tn), lambda l, j: (l, 0, j)),
                  pl.BlockSpec((1, 1, tn), lambda l, j: (l, 0, j))],
        out_specs=[pl.BlockSpec((1, n_rows, tn), lambda l, j: (l, 0, j)),
                   pl.BlockSpec((1, nb, 1, tn), lambda l, j: (l, 0, 0, j))],
        out_shape=[jax.ShapeDtypeStruct((depth, n_rows, n), F32),
                   jax.ShapeDtypeStruct((depth, nb, 1, n), F32)],
        scratch_shapes=[pltpu.VMEM((rows, d), BF16)],
        compiler_params=_params("arbitrary", "arbitrary"),
        name="ada",
    )(c_all, w_ada, b_ada.reshape(depth, 1, n))


class _Rows:
    def __init__(self, nseq, T, bb, tT, per_row=False):
        assert nseq % bb == 0 and T % tT == 0 and (nseq == 1 or not per_row)
        self.nseq, self.T, self.bb, self.tT = nseq, T, bb, tT
        self.nt = T // tT
        self.grid = (nseq // bb, self.nt)
        self.ntiles = self.grid[0] * self.nt
        self.per_row = per_row

    def _tile(self, tile):
        nt = self.nt
        return tile if tile is not None else (lambda i, t, *_: i * nt + t)

    def act(self, width, col=0, tile=None):
        nt, fn = self.nt, self._tile(tile)
        return pl.BlockSpec((self.bb, self.tT, width), lambda *g: (fn(*g) // nt, fn(*g) % nt, col))

    def mod(self, layer, k, tile=None):
        nt, fn = self.nt, self._tile(tile)
        if self.per_row:
            return pl.BlockSpec((None, 1, self.tT, D_MODEL), lambda *g: (layer, 0, fn(*g) % nt, k))
        return pl.BlockSpec((None, self.bb, 1, D_MODEL), lambda *g: (layer, fn(*g) // nt, 0, k))

    def seq(self, *tail):
        nz = (0,) * len(tail)
        return pl.BlockSpec((self.bb,) + tuple(tail), lambda i, t, *_: (i,) + nz)


def _modulated(x, g, scale, shift):
    return (_rms(x) * g) * (1.0 + scale) + shift


def _inproj_kernel(x_ref, sh_ref, sc_ref, g_ref, w_ref, z_ref, *, col_chunk):
    bb, tT, d = x_ref.shape
    h = _modulated(x_ref[...], g_ref[...], sc_ref[...], sh_ref[...])
    h = h.reshape(bb * tT, d).astype(BF16)
    n = w_ref.shape[1]
    for c0 in range(0, n, col_chunk):
        c1 = min(n, c0 + col_chunk)
        z_ref[:, :, c0:c1] = _mm(h, w_ref[:, c0:c1]).reshape(bb, tT, c1 - c0)


def _inproj(rows, x, mods, layer, g, w, name):
    n = w.shape[1]
    return pl.pallas_call(
        functools.partial(_inproj_kernel, col_chunk=512),
        grid=rows.grid,
        in_specs=[rows.act(D_MODEL), rows.mod(layer, 0), rows.mod(layer, 1),
                  _layer_block(layer, (1, 1, D_MODEL)),
                  pl.BlockSpec((D_MODEL, n), lambda *_: (0, 0), pipeline_mode=pl.Buffered(1))],
        out_specs=rows.act(n),
        out_shape=jax.ShapeDtypeStruct((rows.nseq, rows.T, n), F32),
        compiler_params=_params("parallel", "parallel"),
        name=name,
    )(x, mods, mods, g.reshape(-1, 1, 1, D_MODEL), w)


def _outproj_kernel(oa_ref, ob_ref, y_ref, gate_ref, w_ref, o_ref):
    bb, tT, d = y_ref.shape
    half = oa_ref.shape[-1]
    oa = oa_ref[...].reshape(bb * tT, half)
    ob = ob_ref[...].reshape(bb * tT, half)
    acc = _mm(oa, w_ref[:half, :]) + _mm(ob, w_ref[half:, :])
    o_ref[...] = y_ref[...] + gate_ref[...] * acc.reshape(bb, tT, d)


def _outproj(rows, oa, ob, y, mods, layer, w, name):
    half = oa.shape[-1]
    return pl.pallas_call(
        _outproj_kernel,
        grid=rows.grid,
        in_specs=[rows.act(half), rows.act(half), rows.act(D_MODEL), rows.mod(layer, 2),
                  _layer_block(layer, (D_MODEL, D_MODEL), resident=True)],
        out_specs=rows.act(D_MODEL),
        out_shape=jax.ShapeDtypeStruct(y.shape, F32),
        compiler_params=_params("parallel", "parallel"),
        name=name,
    )(oa, ob, y, mods, w)


def _ffn_kernel(y_ref, yprev_ref, sh_ref, sc_ref, gt_ref, g_ref, wg_ref, wu_ref, wd_ref, gf_ref, o_ref,
                h_scr, p_scr, acc_scr, *, nf, nsteps, final):
    g = pl.program_id(0)
    f = g % nf
    last = nsteps - 1
    bb, tT, d = y_ref.shape

    def prologue():
        h = _modulated(y_ref[...], g_ref[...], sc_ref[...], sh_ref[...])
        h_scr[...] = h.reshape(bb * tT, d).astype(BF16)

    def up(slot):
        h = h_scr[...]
        a = _mm(h, wg_ref[...])
        b = _mm(h, wu_ref[...])
        p_scr[slot] = (_silu(a) * b).astype(BF16)

    def down(slot):
        return _mm(p_scr[slot], wd_ref[...])

    def epilogue(res):
        out = yprev_ref[...] + gt_ref[...] * (acc_scr[...] + res).reshape(bb, tT, d)
        if final:
            out = _rms(out) * gf_ref[...]
        o_ref[...] = out
        acc_scr[...] = jnp.zeros_like(acc_scr)

    @pl.when(g == 0)
    def _():
        acc_scr[...] = jnp.zeros_like(acc_scr)
        prologue()
        up(0)

    for par in range(2):
        mid = (g > 0) & (g < last) & (g % 2 == par)

        @pl.when(mid & (f != 0))
        def _(par=par):
            acc_scr[...] += down(1 - par)
            up(par)

        @pl.when(mid & (f == 0))
        def _(par=par):
            res = down(1 - par)
            prologue()
            epilogue(res)
            up(par)

    @pl.when(g == last)
    def _():
        epilogue(down((last - 1) % 2))


def _ffn(rows, y, mods, layer, g, wg, wu, wd, g_final, final, name, tf=512):
    nf = D_FF // tf
    m = rows.bb * rows.tT
    nsteps = rows.ntiles * nf + 1
    cur = lambda s: jnp.minimum(s // nf, rows.ntiles - 1)
    prev = lambda s: jnp.maximum(s - 1, 0) // nf
    up_chunk = lambda s: jnp.minimum(s, nsteps - 2) % nf
    down_chunk = lambda s: jnp.maximum(s - 1, 0) % nf
    return pl.pallas_call(
        functools.partial(_ffn_kernel, nf=nf, nsteps=nsteps, final=final),
        grid=(nsteps,),
        in_specs=[rows.act(D_MODEL, tile=cur), rows.act(D_MODEL, tile=prev),
                  rows.mod(layer, 3, cur), rows.mod(layer, 4, cur), rows.mod(layer, 5, prev),
                  _layer_block(layer, (1, 1, D_MODEL)),
                  pl.BlockSpec((None, D_MODEL, tf), lambda s: (layer, 0, up_chunk(s))),
                  pl.BlockSpec((None, D_MODEL, tf), lambda s: (layer, 0, up_chunk(s))),
                  pl.BlockSpec((None, tf, D_MODEL), lambda s: (layer, down_chunk(s), 0)),
                  _const((1, 1, D_MODEL))],
        out_specs=rows.act(D_MODEL, tile=prev),
        out_shape=jax.ShapeDtypeStruct(y.shape, F32),
        scratch_shapes=[pltpu.VMEM((m, D_MODEL), BF16), pltpu.VMEM((2, m, tf), BF16),
                        pltpu.VMEM((m, D_MODEL), F32)],
        compiler_params=_params("arbitrary"),
        name=name,
    )(y, y, mods, mods, mods, g.reshape(-1, 1, 1, D_MODEL), wg, wu, wd, g_final.reshape(1, 1, D_MODEL))


CONV_PAD = 32


def _conv_kernel(za_ref, prev_ref, w_ref, cb_ref, g_ref, b_ref, y_ref, st_ref, ubuf, ybuf,
                 *, t_valid, rc, lc):
    t = pl.program_id(1)
    bb, tT, _ = za_ref.shape
    lo = CONV_PAD - CONV_HIST
    assert lc == LANES
    nlc = D_CONV // lc

    @pl.when(t == 0)
    def _():
        for b in range(bb):
            for ci in range(nlc):
                ubuf[b * nlc + ci, lo:CONV_PAD, :] = prev_ref[b, :, ci * lc:(ci + 1) * lc]

    @pl.when(t > 0)
    def _():
        ubuf[:, lo:CONV_PAD, :] = ubuf[:, tT + lo:tT + CONV_PAD, :]

    for b in range(bb):
        for ci in range(nlc):
            c0 = ci * lc
            ubuf[b * nlc + ci, CONV_PAD:CONV_PAD + tT, :] = (
                za_ref[b, :, c0:c0 + lc] * _sigmoid(za_ref[b, :, D_CONV + c0:D_CONV + c0 + lc]))

    for b in range(bb):
        def row_chunk(i, carry, b=b):
            r0 = pl.multiple_of(i * rc, SUBLANES)
            for ci in range(nlc):
                c0 = ci * lc
                acc = jnp.broadcast_to(cb_ref[:, c0:c0 + lc], (rc, lc))
                for j in range(CONV_WIDTH):
                    acc = acc + w_ref[j:j + 1, c0:c0 + lc] * ubuf[b * nlc + ci, pl.ds(r0 + lo + j, rc), :]
                ybuf[b, pl.ds(r0, rc), c0:c0 + lc] = acc
            return carry
        lax.fori_loop(0, tT // rc, row_chunk, 0)

    y = _layer_norm(ybuf[...], g_ref[...], b_ref[...])
    y_ref[...] = _silu(y).astype(y_ref.dtype)

    @pl.when(t == pl.num_programs(1) - 1)
    def _():
        for b in range(bb):
            for ci in range(nlc):
                st_ref[b, :, ci * lc:(ci + 1) * lc] = ubuf[b * nlc + ci, t_valid + lo:t_valid + CONV_PAD, :]


def _conv(rows, z, prev, conv_w, conv_b, ln_g, ln_b, t_valid, name):
    bb, tT = rows.bb, rows.tT
    rc = min(tT, 64)
    return pl.pallas_call(
        functools.partial(_conv_kernel, t_valid=t_valid, rc=rc, lc=LANES),
        grid=rows.grid,
        in_specs=[rows.act(2 * D_CONV, 0), rows.seq(CONV_HIST, D_CONV),
                  _const((CONV_WIDTH, D_CONV)), _const((1, D_CONV)), _const((1, D_CONV)), _const((1, D_CONV))],
        out_specs=[rows.act(D_CONV), rows.seq(CONV_HIST, D_CONV)],
        out_shape=[jax.ShapeDtypeStruct((rows.nseq, rows.T, D_CONV), BF16),
                   jax.ShapeDtypeStruct((rows.nseq, CONV_HIST, D_CONV), F32)],
        scratch_shapes=[pltpu.VMEM((bb * (D_CONV // LANES), CONV_PAD + tT, LANES), F32),
                        pltpu.VMEM((bb, tT, D_CONV), F32)],
        compiler_params=_params("parallel", "arbitrary"),
        name=name,
    )(z, prev, conv_w, conv_b.reshape(1, -1), ln_g.reshape(1, -1), ln_b.reshape(1, -1))


def _inproj_conv_kernel(x_ref, sh_ref, sc_ref, g_ref, w_ref, prev_ref, cw_ref, cb_ref, lg_ref, lb_ref,
                        z_ref, y_ref, st_ref, ubuf, ybuf, *, rc, col_chunk):
    t = pl.program_id(1)
    _, tT, d = x_ref.shape
    lo = CONV_PAD - CONV_HIST
    lc = LANES
    nlc = D_CONV // lc
    n_rest = z_ref.shape[-1]

    @pl.when(t == 0)
    def _():
        for ci in range(nlc):
            ubuf[ci, lo:CONV_PAD, :] = prev_ref[0, :, ci * lc:(ci + 1) * lc]

    @pl.when(t > 0)
    def _():
        ubuf[:, lo:CONV_PAD, :] = ubuf[:, tT + lo:tT + CONV_PAD, :]

    h = _modulated(x_ref[...], g_ref[...], sc_ref[...], sh_ref[...]).reshape(tT, d).astype(BF16)

    for c0 in range(0, D_CONV, col_chunk):
        u = _mm(h, w_ref[:, c0:c0 + col_chunk]) * _sigmoid(_mm(h, w_ref[:, D_CONV + c0:D_CONV + c0 + col_chunk]))
        for k in range(col_chunk // lc):
            ubuf[c0 // lc + k, CONV_PAD:CONV_PAD + tT, :] = u[:, k * lc:(k + 1) * lc]

    def project_rest(c0):
        c1 = min(n_rest, c0 + col_chunk)
        z_ref[0, :, c0:c1] = _mm(h, w_ref[:, 2 * D_CONV + c0:2 * D_CONV + c1])

    def conv_rows(r0):
        for ci in range(nlc):
            c0 = ci * lc
            acc = jnp.broadcast_to(cb_ref[:, c0:c0 + lc], (rc, lc))
            for j in range(CONV_WIDTH):
                acc = acc + cw_ref[j:j + 1, c0:c0 + lc] * ubuf[ci, r0 + lo + j:r0 + lo + j + rc, :]
            ybuf[r0:r0 + rc, c0:c0 + lc] = acc
        y = _layer_norm(ybuf[r0:r0 + rc, :], lg_ref[...], lb_ref[...])
        y_ref[0, r0:r0 + rc, :] = _silu(y).astype(y_ref.dtype)

    rest = list(range(0, n_rest, col_chunk))
    conv = list(range(0, tT, rc))
    while rest or conv:
        if rest:
            project_rest(rest.pop(0))
        if conv:
            conv_rows(conv.pop(0))

    @pl.when(t == pl.num_programs(1) - 1)
    def _():
        for ci in range(nlc):
            st_ref[0, :, ci * lc:(ci + 1) * lc] = ubuf[ci, tT + lo:tT + CONV_PAD, :]


def _inproj_conv(rows, x, mods, layer, g, w, prev, conv_w, conv_b, ln_g, ln_b, name):
    assert rows.bb == 1
    tT = rows.tT
    n_rest = w.shape[1] - 2 * D_CONV
    return pl.pallas_call(
        functools.partial(_inproj_conv_kernel, rc=32, col_chunk=512),
        grid=rows.grid,
        in_specs=[rows.act(D_MODEL), rows.mod(layer, 0), rows.mod(layer, 1),
                  _layer_block(layer, (1, 1, D_MODEL)),
                  pl.BlockSpec(w.shape, lambda *_: (0, 0), pipeline_mode=pl.Buffered(1)),
                  rows.seq(CONV_HIST, D_CONV),
                  _const((CONV_WIDTH, D_CONV)), _const((1, D_CONV)), _const((1, D_CONV)), _const((1, D_CONV))],
        out_specs=[rows.act(n_rest), rows.act(D_CONV), rows.seq(CONV_HIST, D_CONV)],
        out_shape=[jax.ShapeDtypeStruct((rows.nseq, rows.T, n_rest), F32),
                   jax.ShapeDtypeStruct((rows.nseq, rows.T, D_CONV), BF16),
                   jax.ShapeDtypeStruct((rows.nseq, CONV_HIST, D_CONV), F32)],
        scratch_shapes=[pltpu.VMEM((D_CONV // LANES, CONV_PAD + tT, LANES), F32),
                        pltpu.VMEM((tT, D_CONV), F32)],
        compiler_params=_params("parallel", "arbitrary"),
        name=name,
    )(x, mods, mods, g.reshape(-1, 1, 1, D_MODEL), w, prev, conv_w, conv_b.reshape(1, -1),
      ln_g.reshape(1, -1), ln_b.reshape(1, -1))


def _segment_cumsum(x, seg):
    pos = lax.broadcasted_iota(jnp.int32, x.shape, 0) % seg
    s = 1
    while s < seg:
        x = x + jnp.where(pos >= s, pltpu.roll(x, s, 0), 0.0)
        s *= 2
    return x


def _gla_kernel(*refs, nseg, seg, t_valid, has_state):
    if has_state:
        (zq_ref, zk_ref, zv_ref, zg_ref, zr_ref, wg_ref, bg_ref, gn_ref, s0_ref,
         y_ref, st_ref, st_scr) = refs
    else:
        (zq_ref, zk_ref, zv_ref, zg_ref, zr_ref, wg_ref, bg_ref, gn_ref,
         y_ref, st_ref, st_scr) = refs
    t = pl.program_id(1)
    bb, tT, _ = zq_ref.shape
    R = ROW_BLOCK
    assert nseg * seg == R and (bb * tT) % R == 0

    @pl.when(t == 0)
    def _():
        for g in range(nseg):
            for h in range(GLA_HEADS):
                st_scr[g, h] = s0_ref[g, h].T if has_state else jnp.zeros((GLA_HV, GLA_HK), F32)

    def rows_of(ref, c):
        if nseg == 1:
            return ref[0, c * R:(c + 1) * R, :]
        return ref[...].reshape(R, ref.shape[-1])

    row = lax.broadcasted_iota(jnp.int32, (R, R), 0)
    col = lax.broadcasted_iota(jnp.int32, (R, R), 1)
    causal = (row >= col) & ((row // seg) == (col // seg))
    pos = lax.broadcasted_iota(jnp.int32, (R, GLA_DK), 0) % seg
    colseg = lax.broadcasted_iota(jnp.int32, (GLA_HV, R), 1) // seg

    for c in range(bb * tT // R):
        zr = rows_of(zr_ref, c).astype(BF16)
        la = _log_sigmoid(_mm(zr, wg_ref[...]) + bg_ref[...]) * (1.0 / GLA_GATE_NORM)
        q = rows_of(zq_ref, c) * (GLA_HK ** -0.5)
        k = rows_of(zk_ref, c)
        v = rows_of(zv_ref, c)
        zg = rows_of(zg_ref, c)
        if t_valid < seg:
            la = jnp.where(pos < t_valid, la, 0.0)
            k = jnp.where(pos < t_valid, k, 0.0)
        bcum = _segment_cumsum(la, seg)
        b3 = bcum.reshape(nseg, seg, GLA_DK)
        bl3 = b3[:, seg - 1:seg, :]
        k_last = k * jnp.exp(bl3 - b3).reshape(R, GLA_DK)
        dec3 = jnp.exp(bl3)
        q_dec = q * jnp.exp(bcum)
        bmid = b3[:, seg // 2 - 1:seg // 2, :] if nseg == 1 else jnp.zeros_like(bl3)
        q_in = (q * jnp.exp(b3 - bmid).reshape(R, GLA_DK)).astype(BF16)
        k_in = (k * jnp.exp(bmid - b3).reshape(R, GLA_DK)).astype(BF16)
        q_dec = q_dec.astype(BF16)
        k_last = k_last.astype(BF16)
        for h in range(GLA_HEADS):
            ks = slice(h * GLA_HK, (h + 1) * GLA_HK)
            vs = slice(h * GLA_HV, (h + 1) * GLA_HV)
            v_h = v[:, vs]
            att = jnp.where(causal, _mm_nt(q_in[:, ks], k_in[:, ks]), 0.0)
            o = _mm(att.astype(BF16), v_h.astype(BF16))
            vt = v_h.T
            inter = []
            for g in range(nseg):
                st = st_scr[g, h]
                inter.append(_mm_nt(q_dec[g * seg:(g + 1) * seg, ks], st.astype(BF16)))
                vt_g = vt if nseg == 1 else jnp.where(colseg == g, vt, 0.0)
                st_scr[g, h] = st * dec3[g, :, ks] + _mm(vt_g.astype(BF16), k_last[:, ks])
            o = o + (inter[0] if nseg == 1 else jnp.concatenate(inter, axis=0))
            o = _rms(o) * gn_ref[...] * _silu(zg[:, vs])
            o = o.astype(y_ref.dtype)
            if nseg == 1:
                y_ref[0, c * R:(c + 1) * R, vs] = o
            else:
                y_ref[:, :, vs] = o.reshape(bb, tT, GLA_HV)

    @pl.when(t == pl.num_programs(1) - 1)
    def _():
        for g in range(nseg):
            for h in range(GLA_HEADS):
                st_ref[g, h] = st_scr[g, h].T


def _gla(rows, z, col0, s0, w_gate, b_gate, g_norm, t_valid, name):
    bb = rows.bb
    nseg = bb if bb > 1 else 1
    seg = ROW_BLOCK // nseg
    has_state = s0 is not None
    wg = jnp.pad(w_gate, ((0, LANES - GLA_RANK), (0, 0))).astype(BF16)
    state_spec = rows.seq(GLA_HEADS, GLA_HK, GLA_HV)
    in_specs = [rows.act(GLA_DK, col0 // GLA_DK), rows.act(GLA_DK, col0 // GLA_DK + 1),
                rows.act(GLA_DV, (col0 + 2 * GLA_DK) // GLA_DV),
                rows.act(GLA_DV, (col0 + 2 * GLA_DK) // GLA_DV + 1),
                rows.act(LANES, (col0 + 2 * GLA_DK + 2 * GLA_DV) // LANES),
                _const((LANES, GLA_DK)), _const((1, GLA_DK)), _const((1, GLA_HV))]
    args = [z, z, z, z, z, wg, b_gate.reshape(1, -1), g_norm.reshape(1, -1)]
    if has_state:
        in_specs.append(state_spec)
        args.append(s0)
    return pl.pallas_call(
        functools.partial(_gla_kernel, nseg=nseg, seg=seg, t_valid=t_valid, has_state=has_state),
        grid=rows.grid,
        in_specs=in_specs,
        out_specs=[rows.act(GLA_DV), state_spec],
        out_shape=[jax.ShapeDtypeStruct((rows.nseq, rows.T, GLA_DV), BF16),
                   jax.ShapeDtypeStruct((rows.nseq, GLA_HEADS, GLA_HK, GLA_HV), F32)],
        scratch_shapes=[pltpu.VMEM((nseg, GLA_HEADS, GLA_HV, GLA_HK), F32)],
        compiler_params=_params("parallel", "arbitrary"),
        name=name,
    )(*args)


def _sgu_kernel(zu_ref, zv_ref, w_ref, bias_ref, g_ref, b_ref, y_ref, *maybe_v_ref):
    bb, tT, _ = zu_ref.shape
    R = ROW_BLOCK
    for c in range(bb * tT // R):
        if bb == 1:
            zu = zu_ref[0, c * R:(c + 1) * R, :]
            zv = zv_ref[0, c * R:(c + 1) * R, :]
        else:
            zu = zu_ref[...].reshape(R, D_SGU)
            zv = zv_ref[...].reshape(R, D_SGU)
        v = _layer_norm(_gelu_tanh(zv), g_ref[...], b_ref[...])
        vb = v.astype(BF16)
        mixed = jnp.concatenate(
            [_mm(w_ref[g], vb[:, g * SGU_GD:(g + 1) * SGU_GD]) for g in range(SGU_GROUPS)], axis=1)
        out = (_gelu_tanh(zu) * (mixed + bias_ref[...])).astype(y_ref.dtype)
        if bb == 1:
            y_ref[0, c * R:(c + 1) * R, :] = out
        else:
            y_ref[...] = out.reshape(bb, tT, D_SGU)
        for v_ref in maybe_v_ref:
            if bb == 1:
                v_ref[0, c * R:(c + 1) * R, :] = v
            else:
                v_ref[...] = v.reshape(bb, tT, D_SGU)


def _sgu(rows, z, w_blocks, bias_rows, ln_g, ln_b, want_v, name):
    out_specs = [rows.act(D_SGU)]
    out_shape = [jax.ShapeDtypeStruct((rows.nseq, rows.T, D_SGU), BF16)]
    if want_v:
        out_specs.append(rows.act(D_SGU))
        out_shape.append(jax.ShapeDtypeStruct((rows.nseq, rows.T, D_SGU), F32))
    return pl.pallas_call(
        _sgu_kernel,
        grid=rows.grid,
        in_specs=[rows.act(D_SGU, 0), rows.act(D_SGU, 1),
                  _const((SGU_GROUPS, ROW_BLOCK, ROW_BLOCK)), _const((ROW_BLOCK, D_SGU)),
                  _const((1, D_SGU)), _const((1, D_SGU))],
        out_specs=out_specs,
        out_shape=out_shape,
        compiler_params=_params("parallel", "parallel"),
        name=name,
    )(z, z, w_blocks, bias_rows, ln_g.reshape(1, -1), ln_b.reshape(1, -1))


def _swa_kernel(sink_ref, q_ref, kc_ref, vc_ref, kp_ref, vp_ref, y_ref, *, prev_from_grid):
    bb, rq, _ = q_ref.shape
    W = WINDOW
    first_key = jnp.where(pl.program_id(1) > 0, 0, W) if prev_from_grid else 0
    tq = lax.broadcasted_iota(jnp.int32, (rq, 2 * W), 0)
    sk = lax.broadcasted_iota(jnp.int32, (rq, 2 * W), 1)
    dist_i = tq + W - sk
    valid = (dist_i >= 0) & (dist_i < W) & (sk >= first_key)
    dist = dist_i.astype(F32)
    scale = ATT_HD ** -0.5
    for b in range(bb):
        q = q_ref[b]
        kc, vc = kc_ref[b], vc_ref[b]
        if rq < W:
            pad = jnp.zeros((W - rq, KV_WIDTH), F32)
            kc = jnp.concatenate([kc, pad], axis=0)
            vc = jnp.concatenate([vc, pad], axis=0)
        kp, vp = kp_ref[b], vp_ref[b]
        for g in range(ATT_KV_HEADS):
            gs = slice(g * ATT_HD, (g + 1) * ATT_HD)
            qg = jnp.concatenate(
                [q[:, (g * ATT_REP + r) * ATT_HD:(g * ATT_REP + r + 1) * ATT_HD] for r in range(ATT_REP)],
                axis=0).astype(BF16)
            s_all = jnp.concatenate([_mm_nt(qg, kp[:, gs].astype(BF16)),
                                     _mm_nt(qg, kc[:, gs].astype(BF16))], axis=1) * scale
            probs, inv = [], []
            for r in range(ATT_REP):
                h = g * ATT_REP + r
                sink = sink_ref[h]
                s = jnp.where(valid, s_all[r * rq:(r + 1) * rq, :] - ALIBI_SLOPES[h] * dist, NEG_INF)
                m = jnp.maximum(jnp.max(s, axis=-1, keepdims=True), sink)
                p = jnp.exp(s - m)
                inv.append(1.0 / (jnp.sum(p, axis=-1, keepdims=True) + jnp.exp(sink - m)))
                probs.append(p.astype(BF16))
            p_all = jnp.concatenate(probs, axis=0)
            o = _mm(p_all[:, :W], vp[:, gs].astype(BF16)) + _mm(p_all[:, W:], vc[:, gs].astype(BF16))
            for r in range(ATT_REP):
                h = g * ATT_REP + r
                y_ref[b, :, h * ATT_HD:(h + 1) * ATT_HD] = (o[r * rq:(r + 1) * rq, :] * inv[r]).astype(y_ref.dtype)


def _swa(rows, z, sinks, k_prev, v_prev, name):
    bb, tT = rows.bb, rows.tT
    kcol = (2 * D_SGU + ATT_WIDTH) // KV_WIDTH
    from_grid = k_prev is None
    if from_grid:
        assert tT == WINDOW and bb == 1
        kp_spec = pl.BlockSpec((1, WINDOW, KV_WIDTH), lambda i, t: (i, jnp.maximum(t - 1, 0), kcol))
        vp_spec = pl.BlockSpec((1, WINDOW, KV_WIDTH), lambda i, t: (i, jnp.maximum(t - 1, 0), kcol + 1))
        k_prev, v_prev = z, z
    else:
        kp_spec = vp_spec = rows.seq(WINDOW, KV_WIDTH)
    return pl.pallas_call(
        functools.partial(_swa_kernel, prev_from_grid=from_grid),
        grid=rows.grid,
        in_specs=[pl.BlockSpec(memory_space=pltpu.SMEM),
                  rows.act(ATT_WIDTH, 2 * D_SGU // ATT_WIDTH), rows.act(KV_WIDTH, kcol),
                  rows.act(KV_WIDTH, kcol + 1), kp_spec, vp_spec],
        out_specs=rows.act(ATT_WIDTH),
        out_shape=jax.ShapeDtypeStruct((rows.nseq, rows.T, ATT_WIDTH), BF16),
        compiler_params=_params("parallel", "arbitrary"),
        name=name,
    )(sinks, z, z, z, k_prev, v_prev)


def kernel(x_prompt, x_sample, c_prompt, c_sample, state_conv, state_gla, cache_k, cache_v, w_ada, b_ada, g_mix, g_ffn, w_in_ab, conv_w, conv_b, conv_ln_g, conv_ln_b, gla_w_gate, gla_b_gate, gla_g_norm, w_in_cd, sgu_w, sgu_b, sgu_ln_g, sgu_ln_b, att_sinks, w_out, w_ffn_gate, w_ffn_up, w_ffn_down, g_final):
    nb, seq, d = x_prompt.shape
    ndb, dseq, _ = x_sample.shape
    ns = ndb * dseq
    assert d == D_MODEL and dseq <= SAMPLE_ROWS and seq % 512 == 0 and ns % 256 == 0
    sb = ROW_BLOCK // SAMPLE_ROWS
    assert ndb % sb == 0

    w_ab = jnp.pad(w_in_ab, ((0, 0), (0, IN_AB_PAD - IN_AB))).astype(BF16)
    w_cd = w_in_cd.astype(BF16)
    w_o = w_out.astype(BF16)
    w_fg, w_fu, w_fd = w_ffn_gate.astype(BF16), w_ffn_up.astype(BF16), w_ffn_down.astype(BF16)

    tri = jnp.tril(jnp.ones((ROW_BLOCK, ROW_BLOCK), bool))
    sgu_wp = jnp.where(tri, sgu_w, 0.0).astype(BF16)
    sgu_ws = jnp.where(tri, sgu_w, 0.0)[:, :SAMPLE_ROWS, :SAMPLE_ROWS]
    sgu_ws = jnp.einsum("ab,gts->gatbs", jnp.eye(sb, dtype=F32), sgu_ws).reshape(
        SGU_GROUPS, ROW_BLOCK, ROW_BLOCK).astype(BF16)
    sgu_bias_p = jnp.repeat(sgu_b.T, SGU_GD, axis=1)
    sgu_bias_s = jnp.tile(sgu_bias_p[:SAMPLE_ROWS], (sb, 1))

    c_all = jnp.concatenate([jnp.repeat(c_sample, dseq, axis=0), c_prompt], axis=0)
    mods_rows, mods_p = _ada(c_all, ns, w_ada, b_ada)
    mods_s = mods_rows.reshape(mods_rows.shape[0], 1, ns, N_MODS * D_MODEL)

    rp_dense = _Rows(nb, seq, 1, 512)
    rs_dense = _Rows(1, ns, 1, 256, per_row=True)
    rp_in = _Rows(nb, seq, 1, 256)
    rs_in = _Rows(1, ns, 1, 256, per_row=True)
    rp_mix = _Rows(nb, seq, 1, 256)
    rp_att = _Rows(nb, seq, 1, WINDOW)
    rs_mix = _Rows(ndb, SAMPLE_ROWS, sb, SAMPLE_ROWS)

    def to_mixer(z):
        z = z.reshape(ndb, dseq, z.shape[-1])
        return jnp.pad(z, ((0, 0), (0, SAMPLE_ROWS - dseq), (0, 0)))

    def to_dense(o):
        return o[:, :dseq].reshape(1, ns, o.shape[-1])

    yp, ys = x_prompt, x_sample.reshape(1, ns, d)

    conv0 = jnp.zeros((nb, CONV_HIST, D_CONV), F32)
    zp, oa_p, st_conv_p = _inproj_conv(rp_in, yp, mods_p, 0, g_mix, w_ab, conv0, conv_w, conv_b,
                                       conv_ln_g, conv_ln_b, "inproj_conv_p")
    zs = to_mixer(_inproj(rs_in, ys, mods_s, 0, g_mix, w_ab, "inproj_ab_s"))
    oa_s, st_conv_s = _conv(rs_mix, zs, state_conv, conv_w, conv_b, conv_ln_g, conv_ln_b, dseq, "conv_s")
    ob_p, st_gla_p = _gla(rp_mix, zp, 0, None, gla_w_gate, gla_b_gate, gla_g_norm, ROW_BLOCK, "gla_p")
    ob_s, st_gla_s = _gla(rs_mix, zs, 2 * D_CONV, state_gla, gla_w_gate, gla_b_gate, gla_g_norm, dseq, "gla_s")
    yp = _outproj(rp_dense, oa_p, ob_p, yp, mods_p, 0, w_o, "outproj0_p")
    ys = _outproj(rs_dense, to_dense(oa_s), to_dense(ob_s), ys, mods_s, 0, w_o, "outproj0_s")
    yp = _ffn(rp_dense, yp, mods_p, 0, g_ffn, w_fg, w_fu, w_fd, g_final, False, "ffn0_p")
    ys = _ffn(rs_dense, ys, mods_s, 0, g_ffn, w_fg, w_fu, w_fd, g_final, False, "ffn0_s")

    zp = _inproj(rp_in, yp, mods_p, 1, g_mix, w_cd, "inproj_cd_p")
    zs = to_mixer(_inproj(rs_in, ys, mods_s, 1, g_mix, w_cd, "inproj_cd_s"))
    (oc_p,) = _sgu(rp_mix, zp, sgu_wp, sgu_bias_p, sgu_ln_g, sgu_ln_b, False, "sgu_p")
    oc_s, sgu_v = _sgu(rs_mix, zs, sgu_ws, sgu_bias_s, sgu_ln_g, sgu_ln_b, True, "sgu_s")
    wb = cache_k.shape[1]
    od_p = _swa(rp_att, zp, att_sinks, None, None, "swa_p")
    od_s = _swa(rs_mix, zs, att_sinks, cache_k.reshape(ndb, wb, KV_WIDTH),
                cache_v.reshape(ndb, wb, KV_WIDTH), "swa_s")
    yp = _outproj(rp_dense, oc_p, od_p, yp, mods_p, 1, w_o, "outproj1_p")
    ys = _outproj(rs_dense, to_dense(oc_s), to_dense(od_s), ys, mods_s, 1, w_o, "outproj1_s")
    yp = _ffn(rp_dense, yp, mods_p, 1, g_ffn, w_fg, w_fu, w_fd, g_final, True, "ffn1_p")
    ys = _ffn(rs_dense, ys, mods_s, 1, g_ffn, w_fg, w_fu, w_fd, g_final, True, "ffn1_s")

    k0 = 2 * D_SGU + ATT_WIDTH
    kv_shape = (ATT_KV_HEADS, ATT_HD)
    k_new_p = zp[:, seq - wb:, k0:k0 + KV_WIDTH].reshape(nb, wb, *kv_shape)
    v_new_p = zp[:, seq - wb:, k0 + KV_WIDTH:k0 + 2 * KV_WIDTH].reshape(nb, wb, *kv_shape)
    k_new_s = zs[:, :dseq, k0:k0 + KV_WIDTH].reshape(ndb, dseq, *kv_shape)
    v_new_s = zs[:, :dseq, k0 + KV_WIDTH:k0 + 2 * KV_WIDTH].reshape(ndb, dseq, *kv_shape)
    k_s = jnp.concatenate([cache_k, k_new_s], axis=1)[:, -wb:]
    v_s = jnp.concatenate([cache_v, v_new_s], axis=1)[:, -wb:]
    return (yp, ys.reshape(ndb, dseq, d), st_conv_p, st_conv_s, st_gla_p, st_gla_s, sgu_v[:, :dseq],
            k_new_p, k_s, v_new_p, v_s)
```

```python
import functools

import jax
import jax.numpy as jnp
from jax import lax
from jax.experimental import pallas as pl
from jax.experimental.pallas import tpu as pltpu

F32 = jnp.float32
BF16 = jnp.bfloat16

D_MODEL = 2048
NORM_EPS = 1e-6
NEG_INF = -1e30
D_CONV = 1024
CONV_WIDTH = 31
CONV_HIST = CONV_WIDTH - 1
GLA_HEADS = 4
GLA_DV = 1024
GLA_DK = 512
GLA_HK = 128
GLA_HV = 256
GLA_RANK = 16
GLA_GATE_NORM = 16.0
D_SGU = 1024
SGU_GROUPS = 4
SGU_GD = 256
ATT_HD = 64
ATT_HEADS = 16
ATT_KV_HEADS = 2
ATT_REP = 8
ATT_WIDTH = 1024
KV_WIDTH = 128
WINDOW = 128
D_FF = 5632
IN_AB = 5136
IN_AB_PAD = 5248
IN_CD = 3328
N_MODS = 6

LANES = 128
SUBLANES = 8
SAMPLE_ROWS = SUBLANES
ROW_BLOCK = 128
FFN_CHUNK = 512
VMEM_LIMIT = 56 * 1024 * 1024

ALIBI_SLOPES = tuple(2.0 ** (-8.0 * (h + 1) / ATT_HEADS) for h in range(ATT_HEADS))


def _mm(a, b):
    return jnp.dot(a, b, preferred_element_type=F32)


def _mm_nt(a, b):
    return lax.dot_general(a, b, (((1,), (1,)), ((), ())), preferred_element_type=F32)


def _sigmoid(x):
    return 1.0 / (1.0 + jnp.exp(-x))


def _silu(x):
    return x * _sigmoid(x)


def _gelu_tanh(x):
    c = 0.7978845608028654
    return 0.5 * x * (1.0 + jnp.tanh(c * (x + 0.044715 * (x * x * x))))


def _log_sigmoid(x):
    return jnp.minimum(x, 0.0) - jnp.log(1.0 + jnp.exp(-jnp.abs(x)))


def _rms(x):
    return x * lax.rsqrt(jnp.mean(x * x, axis=-1, keepdims=True) + NORM_EPS)


def _layer_norm(x, g, b):
    mu = jnp.mean(x, axis=-1, keepdims=True)
    xc = x - mu
    return xc * lax.rsqrt(jnp.mean(xc * xc, axis=-1, keepdims=True) + NORM_EPS) * g + b


def _params(*sem):
    return pltpu.CompilerParams(dimension_semantics=sem, vmem_limit_bytes=VMEM_LIMIT)


def _const(shape):
    nd = len(shape)
    return pl.BlockSpec(shape, lambda *_: (0,) * nd)


def _layer_block(layer, tail, resident=False):
    nz = (0,) * len(tail)
    mode = dict(pipeline_mode=pl.Buffered(1)) if resident else {}
    return pl.BlockSpec((None,) + tuple(tail), lambda *_: (layer,) + nz, **mode)


def _ada_kernel(c_ref, w_ref, b_ref, rows_ref, seq_ref, a_scr):
    @pl.when((pl.program_id(0) == 0) & (pl.program_id(1) == 0))
    def _():
        a_scr[...] = _silu(c_ref[...]).astype(BF16)

    res = _mm(a_scr[...], w_ref[0].astype(BF16)) + b_ref[0]
    ns = rows_ref.shape[1]
    rows_ref[0] = res[:ns]
    for r in range(seq_ref.shape[1]):
        seq_ref[0, r] = res[ns + r:ns + r + 1]


def _ada(c_all, n_rows, w_ada, b_ada, tn=1024):
    depth, d, n = w_ada.shape
    rows = c_all.shape[0]
    nb = rows - n_rows
    return pl.pallas_call(
        _ada_kernel,
        grid=(depth, n // tn),
        in_specs=[pl.BlockSpec((rows, d), lambda l, j: (0, 0)),
                  pl.BlockSpec((1, d, tn), lambda l, j: (l, 0, j)),
                  pl.BlockSpec((1, 1, tn), lambda l, j: (l, 0, j))],
        out_specs=[pl.BlockSpec((1, n_rows, tn), lambda l, j: (l, 0, j)),
                   pl.BlockSpec((1, nb, 1, tn), lambda l, j: (l, 0, 0, j))],
        out_shape=[jax.ShapeDtypeStruct((depth, n_rows, n), F32),
                   jax.ShapeDtypeStruct((depth, nb, 1, n), F32)],
        scratch_shapes=[pltpu.VMEM((rows, d), BF16)],
        compiler_params=_params("arbitrary", "arbitrary"),
        name="ada",
    )(c_all, w_ada, b_ada.reshape(depth, 1, n))


class _Rows:
    def __init__(self, nseq, T, bb, tT, per_row=False):
        assert nseq % bb == 0 and T % tT == 0 and (nseq == 1 or not per_row)
        self.nseq, self.T, self.bb, self.tT = nseq, T, bb, tT
        self.nt = T // tT
        self.grid = (nseq // bb, self.nt)
        self.ntiles = self.grid[0] * self.nt
        self.per_row = per_row

    def _tile(self, tile):
        nt = self.nt
        return tile if tile is not None else (lambda i, t, *_: i * nt + t)

    def act(self, width, col=0, tile=None):
        nt, fn = self.nt, self._tile(tile)
        return pl.BlockSpec((self.bb, self.tT, width), lambda *g: (fn(*g) // nt, fn(*g) % nt, col))

    def mod(self, layer, k, tile=None):
        nt, fn = self.nt, self._tile(tile)
        if self.per_row:
            return pl.BlockSpec((None, 1, self.tT, D_MODEL), lambda *g: (layer, 0, fn(*g) % nt, k))
        return pl.BlockSpec((None, self.bb, 1, D_MODEL), lambda *g: (layer, fn(*g) // nt, 0, k))

    def seq(self, *tail):
        nz = (0,) * len(tail)
        return pl.BlockSpec((self.bb,) + tuple(tail), lambda i, t, *_: (i,) + nz)


def _modulated(x, g, scale, shift):
    return (_rms(x) * g) * (1.0 + scale) + shift


def _inproj_kernel(x_ref, sh_ref, sc_ref, g_ref, w_ref, z_ref, *, col_chunk):
    bb, tT, d = x_ref.shape
    h = _modulated(x_ref[...], g_ref[...], sc_ref[...], sh_ref[...])
    h = h.reshape(bb * tT, d).astype(BF16)
    n = w_ref.shape[1]
    for c0 in range(0, n, col_chunk):
        c1 = min(n, c0 + col_chunk)
        z_ref[:, :, c0:c1] = _mm(h, w_ref[:, c0:c1]).reshape(bb, tT, c1 - c0)


def _inproj(rows, x, mods, layer, g, w, name):
    n = w.shape[1]
    return pl.pallas_call(
        functools.partial(_inproj_kernel, col_chunk=512),
        grid=rows.grid,
        in_specs=[rows.act(D_MODEL), rows.mod(layer, 0), rows.mod(layer, 1),
                  _layer_block(layer, (1, 1, D_MODEL)),
                  pl.BlockSpec((D_MODEL, n), lambda *_: (0, 0), pipeline_mode=pl.Buffered(1))],
        out_specs=rows.act(n),
        out_shape=jax.ShapeDtypeStruct((rows.nseq, rows.T, n), F32),
        compiler_params=_params("parallel", "parallel"),
        name=name,
    )(x, mods, mods, g.reshape(-1, 1, 1, D_MODEL), w)


def _outproj_kernel(oa_ref, ob_ref, y_ref, gate_ref, w_ref, o_ref):
    bb, tT, d = y_ref.shape
    half = oa_ref.shape[-1]
    oa = oa_ref[...].reshape(bb * tT, half)
    ob = ob_ref[...].reshape(bb * tT, half)
    acc = _mm(oa, w_ref[:half, :]) + _mm(ob, w_ref[half:, :])
    o_ref[...] = y_ref[...] + gate_ref[...] * acc.reshape(bb, tT, d)


def _outproj(rows, oa, ob, y, mods, layer, w, name):
    half = oa.shape[-1]
    return pl.pallas_call(
        _outproj_kernel,
        grid=rows.grid,
        in_specs=[rows.act(half), rows.act(half), rows.act(D_MODEL), rows.mod(layer, 2),
                  _layer_block(layer, (D_MODEL, D_MODEL), resident=True)],
        out_specs=rows.act(D_MODEL),
        out_shape=jax.ShapeDtypeStruct(y.shape, F32),
        compiler_params=_params("parallel", "parallel"),
        name=name,
    )(oa, ob, y, mods, w)


def _ffn_kernel(y_ref, yprev_ref, sh_ref, sc_ref, gt_ref, g_ref, wgu_ref, wd_ref, gf_ref, o_ref,
                h_scr, p_scr, acc_scr, *, nf, nsteps, final):
    g = pl.program_id(0)
    f = g % nf
    last = nsteps - 1
    bb, tT, d = y_ref.shape

    def prologue():
        h = _modulated(y_ref[...], g_ref[...], sc_ref[...], sh_ref[...])
        h_scr[...] = h.reshape(bb * tT, d).astype(BF16)

    def up(slot):
        tf = p_scr.shape[-1]
        ab = _mm(h_scr[...], wgu_ref[...])
        p_scr[slot] = (_silu(ab[:, :tf]) * ab[:, tf:]).astype(BF16)

    def down(slot):
        return _mm(p_scr[slot], wd_ref[...])

    def epilogue(res):
        out = yprev_ref[...] + gt_ref[...] * (acc_scr[...] + res).reshape(bb, tT, d)
        if final:
            out = _rms(out) * gf_ref[...]
        o_ref[...] = out
        acc_scr[...] = jnp.zeros_like(acc_scr)

    @pl.when(g == 0)
    def _():
        acc_scr[...] = jnp.zeros_like(acc_scr)
        prologue()
        up(0)

    for par in range(2):
        mid = (g > 0) & (g < last) & (g % 2 == par)

        @pl.when(mid & (f != 0))
        def _(par=par):
            acc_scr[...] += down(1 - par)
            up(par)

        @pl.when(mid & (f == 0))
        def _(par=par):
            res = down(1 - par)
            prologue()
            epilogue(res)
            up(par)

    @pl.when(g == last)
    def _():
        epilogue(down((last - 1) % 2))


def _ffn_weights(w_gate, w_up, w_down, tf):
    depth, d, dff = w_gate.shape

    def chunked(w):
        return w.reshape(depth, d, dff // tf, tf).transpose(0, 2, 1, 3)

    w_gu = jnp.concatenate([chunked(w_gate), chunked(w_up)], axis=-1).astype(BF16)
    return w_gu, w_down.astype(BF16)


def _ffn(rows, y, mods, layer, g, w_gu, wd, g_final, final, name):
    nf, tf = w_gu.shape[1], w_gu.shape[3] // 2
    m = rows.bb * rows.tT
    nsteps = rows.ntiles * nf + 1
    cur = lambda s: jnp.minimum(s // nf, rows.ntiles - 1)
    prev = lambda s: jnp.maximum(s - 1, 0) // nf
    up_chunk = lambda s: jnp.minimum(s, nsteps - 2) % nf
    down_chunk = lambda s: jnp.maximum(s - 1, 0) % nf
    return pl.pallas_call(
        functools.partial(_ffn_kernel, nf=nf, nsteps=nsteps, final=final),
        grid=(nsteps,),
        in_specs=[rows.act(D_MODEL, tile=cur), rows.act(D_MODEL, tile=prev),
                  rows.mod(layer, 3, cur), rows.mod(layer, 4, cur), rows.mod(layer, 5, prev),
                  _layer_block(layer, (1, 1, D_MODEL)),
                  pl.BlockSpec((None, None, D_MODEL, 2 * tf), lambda s: (layer, up_chunk(s), 0, 0)),
                  pl.BlockSpec((None, tf, D_MODEL), lambda s: (layer, down_chunk(s), 0)),
                  _const((1, 1, D_MODEL))],
        out_specs=rows.act(D_MODEL, tile=prev),
        out_shape=jax.ShapeDtypeStruct(y.shape, F32),
        scratch_shapes=[pltpu.VMEM((m, D_MODEL), BF16), pltpu.VMEM((2, m, tf), BF16),
                        pltpu.VMEM((m, D_MODEL), F32)],
        compiler_params=_params("arbitrary"),
        name=name,
    )(y, y, mods, mods, mods, g.reshape(-1, 1, 1, D_MODEL), w_gu, wd, g_final.reshape(1, 1, D_MODEL))


CONV_PAD = 32


def _conv_kernel(za_ref, prev_ref, w_ref, cb_ref, g_ref, b_ref, y_ref, st_ref, ubuf, ybuf,
                 *, t_valid, rc, lc):
    t = pl.program_id(1)
    bb, tT, _ = za_ref.shape
    lo = CONV_PAD - CONV_HIST
    assert lc == LANES
    nlc = D_CONV // lc

    @pl.when(t == 0)
    def _():
        for b in range(bb):
            for ci in range(nlc):
                ubuf[b * nlc + ci, lo:CONV_PAD, :] = prev_ref[b, :, ci * lc:(ci + 1) * lc]

    @pl.when(t > 0)
    def _():
        ubuf[:, lo:CONV_PAD, :] = ubuf[:, tT + lo:tT + CONV_PAD, :]

    for b in range(bb):
        for ci in range(nlc):
            c0 = ci * lc
            ubuf[b * nlc + ci, CONV_PAD:CONV_PAD + tT, :] = (
                za_ref[b, :, c0:c0 + lc] * _sigmoid(za_ref[b, :, D_CONV + c0:D_CONV + c0 + lc]))

    for b in range(bb):
        def row_chunk(i, carry, b=b):
            r0 = pl.multiple_of(i * rc, SUBLANES)
            for ci in range(nlc):
                c0 = ci * lc
                acc = jnp.broadcast_to(cb_ref[:, c0:c0 + lc], (rc, lc))
                for j in range(CONV_WIDTH):
                    acc = acc + w_ref[j:j + 1, c0:c0 + lc] * ubuf[b * nlc + ci, pl.ds(r0 + lo + j, rc), :]
                ybuf[b, pl.ds(r0, rc), c0:c0 + lc] = acc
            return carry
        lax.fori_loop(0, tT // rc, row_chunk, 0)

    y = _layer_norm(ybuf[...], g_ref[...], b_ref[...])
    y_ref[...] = _silu(y).astype(y_ref.dtype)

    @pl.when(t == pl.num_programs(1) - 1)
    def _():
        for b in range(bb):
            for ci in range(nlc):
                st_ref[b, :, ci * lc:(ci + 1) * lc] = ubuf[b * nlc + ci, t_valid + lo:t_valid + CONV_PAD, :]


def _conv(rows, z, prev, conv_w, conv_b, ln_g, ln_b, t_valid, name):
    bb, tT = rows.bb, rows.tT
    rc = min(tT, 64)
    return pl.pallas_call(
        functools.partial(_conv_kernel, t_valid=t_valid, rc=rc, lc=LANES),
        grid=rows.grid,
        in_specs=[rows.act(2 * D_CONV, 0), rows.seq(CONV_HIST, D_CONV),
                  _const((CONV_WIDTH, D_CONV)), _const((1, D_CONV)), _const((1, D_CONV)), _const((1, D_CONV))],
        out_specs=[rows.act(D_CONV), rows.seq(CONV_HIST, D_CONV)],
        out_shape=[jax.ShapeDtypeStruct((rows.nseq, rows.T, D_CONV), BF16),
                   jax.ShapeDtypeStruct((rows.nseq, CONV_HIST, D_CONV), F32)],
        scratch_shapes=[pltpu.VMEM((bb * (D_CONV // LANES), CONV_PAD + tT, LANES), F32),
                        pltpu.VMEM((bb, tT, D_CONV), F32)],
        compiler_params=_params("parallel", "arbitrary"),
        name=name,
    )(z, prev, conv_w, conv_b.reshape(1, -1), ln_g.reshape(1, -1), ln_b.reshape(1, -1))


def _inproj_conv_kernel(x_ref, sh_ref, sc_ref, g_ref, w_ref, prev_ref, cw_ref, cb_ref, lg_ref, lb_ref,
                        z_ref, y_ref, st_ref, ubuf, ybuf, *, rc, col_chunk):
    t = pl.program_id(1)
    _, tT, d = x_ref.shape
    lo = CONV_PAD - CONV_HIST
    lc = LANES
    nlc = D_CONV // lc
    n_rest = z_ref.shape[-1]

    @pl.when(t == 0)
    def _():
        for ci in range(nlc):
            ubuf[ci, lo:CONV_PAD, :] = prev_ref[0, :, ci * lc:(ci + 1) * lc]

    @pl.when(t > 0)
    def _():
        ubuf[:, lo:CONV_PAD, :] = ubuf[:, tT + lo:tT + CONV_PAD, :]

    h = _modulated(x_ref[...], g_ref[...], sc_ref[...], sh_ref[...]).reshape(tT, d).astype(BF16)

    for c0 in range(0, D_CONV, col_chunk):
        u = _mm(h, w_ref[:, c0:c0 + col_chunk]) * _sigmoid(_mm(h, w_ref[:, D_CONV + c0:D_CONV + c0 + col_chunk]))
        for k in range(col_chunk // lc):
            ubuf[c0 // lc + k, CONV_PAD:CONV_PAD + tT, :] = u[:, k * lc:(k + 1) * lc]

    def project_rest(c0):
        c1 = min(n_rest, c0 + col_chunk)
        z_ref[0, :, c0:c1] = _mm(h, w_ref[:, 2 * D_CONV + c0:2 * D_CONV + c1])

    def conv_rows(r0):
        for ci in range(nlc):
            c0 = ci * lc
            acc = jnp.broadcast_to(cb_ref[:, c0:c0 + lc], (rc, lc))
            for j in range(CONV_WIDTH):
                acc = acc + cw_ref[j:j + 1, c0:c0 + lc] * ubuf[ci, r0 + lo + j:r0 + lo + j + rc, :]
            ybuf[r0:r0 + rc, c0:c0 + lc] = acc
        y = _layer_norm(ybuf[r0:r0 + rc, :], lg_ref[...], lb_ref[...])
        y_ref[0, r0:r0 + rc, :] = _silu(y).astype(y_ref.dtype)

    rest = list(range(0, n_rest, col_chunk))
    conv = list(range(0, tT, rc))
    while rest or conv:
        if rest:
            project_rest(rest.pop(0))
        if conv:
            conv_rows(conv.pop(0))

    @pl.when(t == pl.num_programs(1) - 1)
    def _():
        for ci in range(nlc):
            st_ref[0, :, ci * lc:(ci + 1) * lc] = ubuf[ci, tT + lo:tT + CONV_PAD, :]


def _inproj_conv(rows, x, mods, layer, g, w, prev, conv_w, conv_b, ln_g, ln_b, name):
    assert rows.bb == 1
    tT = rows.tT
    n_rest = w.shape[1] - 2 * D_CONV
    return pl.pallas_call(
        functools.partial(_inproj_conv_kernel, rc=32, col_chunk=512),
        grid=rows.grid,
        in_specs=[rows.act(D_MODEL), rows.mod(layer, 0), rows.mod(layer, 1),
                  _layer_block(layer, (1, 1, D_MODEL)),
                  pl.BlockSpec(w.shape, lambda *_: (0, 0), pipeline_mode=pl.Buffered(1)),
                  rows.seq(CONV_HIST, D_CONV),
                  _const((CONV_WIDTH, D_CONV)), _const((1, D_CONV)), _const((1, D_CONV)), _const((1, D_CONV))],
        out_specs=[rows.act(n_rest), rows.act(D_CONV), rows.seq(CONV_HIST, D_CONV)],
        out_shape=[jax.ShapeDtypeStruct((rows.nseq, rows.T, n_rest), F32),
                   jax.ShapeDtypeStruct((rows.nseq, rows.T, D_CONV), BF16),
                   jax.ShapeDtypeStruct((rows.nseq, CONV_HIST, D_CONV), F32)],
        scratch_shapes=[pltpu.VMEM((D_CONV // LANES, CONV_PAD + tT, LANES), F32),
                        pltpu.VMEM((tT, D_CONV), F32)],
        compiler_params=_params("parallel", "arbitrary"),
        name=name,
    )(x, mods, mods, g.reshape(-1, 1, 1, D_MODEL), w, prev, conv_w, conv_b.reshape(1, -1),
      ln_g.reshape(1, -1), ln_b.reshape(1, -1))


def _segment_cumsum(x, seg):
    pos = lax.broadcasted_iota(jnp.int32, x.shape, 0) % seg
    s = 1
    while s < seg:
        x = x + jnp.where(pos >= s, pltpu.roll(x, s, 0), 0.0)
        s *= 2
    return x


def _gla_kernel(*refs, nseg, seg, t_valid, has_state):
    if has_state:
        (zq_ref, zk_ref, zv_ref, zg_ref, zr_ref, wg_ref, bg_ref, gn_ref, s0_ref,
         y_ref, st_ref, st_scr) = refs
    else:
        (zq_ref, zk_ref, zv_ref, zg_ref, zr_ref, wg_ref, bg_ref, gn_ref,
         y_ref, st_ref, st_scr) = refs
    t = pl.program_id(1)
    bb, tT, _ = zq_ref.shape
    R = ROW_BLOCK
    assert nseg * seg == R and (bb * tT) % R == 0

    @pl.when(t == 0)
    def _():
        for g in range(nseg):
            for h in range(GLA_HEADS):
                st_scr[g, h] = s0_ref[g, h].T if has_state else jnp.zeros((GLA_HV, GLA_HK), F32)

    def rows_of(ref, c):
        if nseg == 1:
            return ref[0, c * R:(c + 1) * R, :]
        return ref[...].reshape(R, ref.shape[-1])

    row = lax.broadcasted_iota(jnp.int32, (R, R), 0)
    col = lax.broadcasted_iota(jnp.int32, (R, R), 1)
    causal = (row >= col) & ((row // seg) == (col // seg))
    pos = lax.broadcasted_iota(jnp.int32, (R, GLA_DK), 0) % seg
    colseg = lax.broadcasted_iota(jnp.int32, (GLA_HV, R), 1) // seg

    for c in range(bb * tT // R):
        zr = rows_of(zr_ref, c).astype(BF16)
        la = _log_sigmoid(_mm(zr, wg_ref[...]) + bg_ref[...]) * (1.0 / GLA_GATE_NORM)
        q = rows_of(zq_ref, c) * (GLA_HK ** -0.5)
        k = rows_of(zk_ref, c)
        v = rows_of(zv_ref, c)
        zg = rows_of(zg_ref, c)
        if t_valid < seg:
            la = jnp.where(pos < t_valid, la, 0.0)
            k = jnp.where(pos < t_valid, k, 0.0)
        bcum = _segment_cumsum(la, seg)
        b3 = bcum.reshape(nseg, seg, GLA_DK)
        bl3 = b3[:, seg - 1:seg, :]
        k_last = k * jnp.exp(bl3 - b3).reshape(R, GLA_DK)
        dec3 = jnp.exp(bl3)
        q_dec = q * jnp.exp(bcum)
        bmid = b3[:, seg // 2 - 1:seg // 2, :] if nseg == 1 else jnp.zeros_like(bl3)
        q_in = (q * jnp.exp(b3 - bmid).reshape(R, GLA_DK)).astype(BF16)
        k_in = (k * jnp.exp(bmid - b3).reshape(R, GLA_DK)).astype(BF16)
        q_dec = q_dec.astype(BF16)
        k_last = k_last.astype(BF16)
        for h in range(GLA_HEADS):
            ks = slice(h * GLA_HK, (h + 1) * GLA_HK)
            vs = slice(h * GLA_HV, (h + 1) * GLA_HV)
            v_h = v[:, vs]
            att = jnp.where(causal, _mm_nt(q_in[:, ks], k_in[:, ks]), 0.0)
            o = _mm(att.astype(BF16), v_h.astype(BF16))
            vt = v_h.T
            inter = []
            for g in range(nseg):
                st = st_scr[g, h]
                inter.append(_mm_nt(q_dec[g * seg:(g + 1) * seg, ks], st.astype(BF16)))
                vt_g = vt if nseg == 1 else jnp.where(colseg == g, vt, 0.0)
                st_scr[g, h] = st * dec3[g, :, ks] + _mm(vt_g.astype(BF16), k_last[:, ks])
            o = o + (inter[0] if nseg == 1 else jnp.concatenate(inter, axis=0))
            o = _rms(o) * gn_ref[...] * _silu(zg[:, vs])
            o = o.astype(y_ref.dtype)
            if nseg == 1:
                y_ref[0, c * R:(c + 1) * R, vs] = o
            else:
                y_ref[:, :, vs] = o.reshape(bb, tT, GLA_HV)

    @pl.when(t == pl.num_programs(1) - 1)
    def _():
        for g in range(nseg):
            for h in range(GLA_HEADS):
                st_ref[g, h] = st_scr[g, h].T


def _gla(rows, z, col0, s0, w_gate, b_gate, g_norm, t_valid, name):
    bb = rows.bb
    nseg = bb if bb > 1 else 1
    seg = ROW_BLOCK // nseg
    has_state = s0 is not None
    wg = jnp.pad(w_gate, ((0, LANES - GLA_RANK), (0, 0))).astype(BF16)
    state_spec = rows.seq(GLA_HEADS, GLA_HK, GLA_HV)
    in_specs = [rows.act(GLA_DK, col0 // GLA_DK), rows.act(GLA_DK, col0 // GLA_DK + 1),
                rows.act(GLA_DV, (col0 + 2 * GLA_DK) // GLA_DV),
                rows.act(GLA_DV, (col0 + 2 * GLA_DK) // GLA_DV + 1),
                rows.act(LANES, (col0 + 2 * GLA_DK + 2 * GLA_DV) // LANES),
                _const((LANES, GLA_DK)), _const((1, GLA_DK)), _const((1, GLA_HV))]
    args = [z, z, z, z, z, wg, b_gate.reshape(1, -1), g_norm.reshape(1, -1)]
    if has_state:
        in_specs.append(state_spec)
        args.append(s0)
    return pl.pallas_call(
        functools.partial(_gla_kernel, nseg=nseg, seg=seg, t_valid=t_valid, has_state=has_state),
        grid=rows.grid,
        in_specs=in_specs,
        out_specs=[rows.act(GLA_DV), state_spec],
        out_shape=[jax.ShapeDtypeStruct((rows.nseq, rows.T, GLA_DV), BF16),
                   jax.ShapeDtypeStruct((rows.nseq, GLA_HEADS, GLA_HK, GLA_HV), F32)],
        scratch_shapes=[pltpu.VMEM((nseg, GLA_HEADS, GLA_HV, GLA_HK), F32)],
        compiler_params=_params("parallel", "arbitrary"),
        name=name,
    )(*args)


def _sgu_kernel(zu_ref, zv_ref, w_ref, bias_ref, g_ref, b_ref, y_ref, *maybe_v_ref):
    bb, tT, _ = zu_ref.shape
    R = ROW_BLOCK
    for c in range(bb * tT // R):
        if bb == 1:
            zu = zu_ref[0, c * R:(c + 1) * R, :]
            zv = zv_ref[0, c * R:(c + 1) * R, :]
        else:
            zu = zu_ref[...].reshape(R, D_SGU)
            zv = zv_ref[...].reshape(R, D_SGU)
        v = _layer_norm(_gelu_tanh(zv), g_ref[...], b_ref[...])
        vb = v.astype(BF16)
        mixed = jnp.concatenate(
            [_mm(w_ref[g], vb[:, g * SGU_GD:(g + 1) * SGU_GD]) for g in range(SGU_GROUPS)], axis=1)
        out = (_gelu_tanh(zu) * (mixed + bias_ref[...])).astype(y_ref.dtype)
        if bb == 1:
            y_ref[0, c * R:(c + 1) * R, :] = out
        else:
            y_ref[...] = out.reshape(bb, tT, D_SGU)
        for v_ref in maybe_v_ref:
            if bb == 1:
                v_ref[0, c * R:(c + 1) * R, :] = v
            else:
                v_ref[...] = v.reshape(bb, tT, D_SGU)


def _sgu(rows, z, w_blocks, bias_rows, ln_g, ln_b, want_v, name):
    out_specs = [rows.act(D_SGU)]
    out_shape = [jax.ShapeDtypeStruct((rows.nseq, rows.T, D_SGU), BF16)]
    if want_v:
        out_specs.append(rows.act(D_SGU))
        out_shape.append(jax.ShapeDtypeStruct((rows.nseq, rows.T, D_SGU), F32))
    return pl.pallas_call(
        _sgu_kernel,
        grid=rows.grid,
        in_specs=[rows.act(D_SGU, 0), rows.act(D_SGU, 1),
                  _const((SGU_GROUPS, ROW_BLOCK, ROW_BLOCK)), _const((ROW_BLOCK, D_SGU)),
                  _const((1, D_SGU)), _const((1, D_SGU))],
        out_specs=out_specs,
        out_shape=out_shape,
        compiler_params=_params("parallel", "parallel"),
        name=name,
    )(z, z, w_blocks, bias_rows, ln_g.reshape(1, -1), ln_b.reshape(1, -1))


def _inproj_sgu_kernel(x_ref, sh_ref, sc_ref, g_ref, w_ref, ws_ref, bias_ref, lg_ref, lb_ref,
                       z_ref, y_ref, zu_scr, zv_scr, *, col_chunk):
    _, tT, d = x_ref.shape
    n_rest = z_ref.shape[-1]
    R = ROW_BLOCK
    h = _modulated(x_ref[...], g_ref[...], sc_ref[...], sh_ref[...]).reshape(tT, d).astype(BF16)
    for c0 in range(0, D_SGU, col_chunk):
        zv_scr[:, c0:c0 + col_chunk] = _mm(h, w_ref[:, D_SGU + c0:D_SGU + c0 + col_chunk])
    for c0 in range(0, D_SGU, col_chunk):
        zu_scr[:, c0:c0 + col_chunk] = _mm(h, w_ref[:, c0:c0 + col_chunk])

    def project_rest(c0):
        c1 = min(n_rest, c0 + col_chunk)
        z_ref[0, :, c0:c1] = _mm(h, w_ref[:, 2 * D_SGU + c0:2 * D_SGU + c1])

    def gate_rows(r0):
        v = _layer_norm(_gelu_tanh(zv_scr[r0:r0 + R, :]), lg_ref[...], lb_ref[...]).astype(BF16)
        mixed = jnp.concatenate(
            [_mm(ws_ref[g], v[:, g * SGU_GD:(g + 1) * SGU_GD]) for g in range(SGU_GROUPS)], axis=1)
        y_ref[0, r0:r0 + R, :] = (_gelu_tanh(zu_scr[r0:r0 + R, :]) * (mixed + bias_ref[...])).astype(y_ref.dtype)

    rest = list(range(0, n_rest, col_chunk))
    gate = list(range(0, tT, R))
    while rest or gate:
        if rest:
            project_rest(rest.pop(0))
        if gate:
            gate_rows(gate.pop(0))


def _inproj_sgu(rows, x, mods, layer, g, w, w_blocks, bias_rows, ln_g, ln_b, name):
    assert rows.bb == 1 and rows.tT % ROW_BLOCK == 0
    tT = rows.tT
    n_rest = w.shape[1] - 2 * D_SGU
    return pl.pallas_call(
        functools.partial(_inproj_sgu_kernel, col_chunk=512),
        grid=rows.grid,
        in_specs=[rows.act(D_MODEL), rows.mod(layer, 0), rows.mod(layer, 1),
                  _layer_block(layer, (1, 1, D_MODEL)),
                  pl.BlockSpec(w.shape, lambda *_: (0, 0), pipeline_mode=pl.Buffered(1)),
                  _const((SGU_GROUPS, ROW_BLOCK, ROW_BLOCK)), _const((ROW_BLOCK, D_SGU)),
                  _const((1, D_SGU)), _const((1, D_SGU))],
        out_specs=[rows.act(n_rest), rows.act(D_SGU)],
        out_shape=[jax.ShapeDtypeStruct((rows.nseq, rows.T, n_rest), F32),
                   jax.ShapeDtypeStruct((rows.nseq, rows.T, D_SGU), BF16)],
        scratch_shapes=[pltpu.VMEM((tT, D_SGU), F32), pltpu.VMEM((tT, D_SGU), F32)],
        compiler_params=_params("parallel", "parallel"),
        name=name,
    )(x, mods, mods, g.reshape(-1, 1, 1, D_MODEL), w, w_blocks, bias_rows,
      ln_g.reshape(1, -1), ln_b.reshape(1, -1))


def _swa_kernel(sink_ref, q_ref, kc_ref, vc_ref, kp_ref, vp_ref, y_ref, *, prev_from_grid):
    bb, rq, _ = q_ref.shape
    W = WINDOW
    first_key = jnp.where(pl.program_id(1) > 0, 0, W) if prev_from_grid else 0
    tq = lax.broadcasted_iota(jnp.int32, (rq, 2 * W), 0)
    sk = lax.broadcasted_iota(jnp.int32, (rq, 2 * W), 1)
    dist_i = tq + W - sk
    valid = (dist_i >= 0) & (dist_i < W) & (sk >= first_key)
    dist = dist_i.astype(F32)
    scale = ATT_HD ** -0.5
    for b in range(bb):
        q = q_ref[b]
        kc, vc = kc_ref[b], vc_ref[b]
        if rq < W:
            pad = jnp.zeros((W - rq, KV_WIDTH), F32)
            kc = jnp.concatenate([kc, pad], axis=0)
            vc = jnp.concatenate([vc, pad], axis=0)
        kp, vp = kp_ref[b], vp_ref[b]
        for g in range(ATT_KV_HEADS):
            gs = slice(g * ATT_HD, (g + 1) * ATT_HD)
            qg = jnp.concatenate(
                [q[:, (g * ATT_REP + r) * ATT_HD:(g * ATT_REP + r + 1) * ATT_HD] for r in range(ATT_REP)],
                axis=0).astype(BF16)
            s_all = jnp.concatenate([_mm_nt(qg, kp[:, gs].astype(BF16)),
                                     _mm_nt(qg, kc[:, gs].astype(BF16))], axis=1) * scale
            probs, inv = [], []
            for r in range(ATT_REP):
                h = g * ATT_REP + r
                sink = sink_ref[h]
                s = jnp.where(valid, s_all[r * rq:(r + 1) * rq, :] - ALIBI_SLOPES[h] * dist, NEG_INF)
                m = jnp.maximum(jnp.max(s, axis=-1, keepdims=True), sink)
                p = jnp.exp(s - m)
                inv.append(1.0 / (jnp.sum(p, axis=-1, keepdims=True) + jnp.exp(sink - m)))
                probs.append(p.astype(BF16))
            p_all = jnp.concatenate(probs, axis=0)
            o = _mm(p_all[:, :W], vp[:, gs].astype(BF16)) + _mm(p_all[:, W:], vc[:, gs].astype(BF16))
            for r in range(ATT_REP):
                h = g * ATT_REP + r
                y_ref[b, :, h * ATT_HD:(h + 1) * ATT_HD] = (o[r * rq:(r + 1) * rq, :] * inv[r]).astype(y_ref.dtype)


def _swa(rows, z, col0, sinks, k_prev, v_prev, name):
    bb, tT = rows.bb, rows.tT
    kcol = (col0 + ATT_WIDTH) // KV_WIDTH
    from_grid = k_prev is None
    if from_grid:
        assert tT == WINDOW and bb == 1
        kp_spec = pl.BlockSpec((1, WINDOW, KV_WIDTH), lambda i, t: (i, jnp.maximum(t - 1, 0), kcol))
        vp_spec = pl.BlockSpec((1, WINDOW, KV_WIDTH), lambda i, t: (i, jnp.maximum(t - 1, 0), kcol + 1))
        k_prev, v_prev = z, z
    else:
        kp_spec = vp_spec = rows.seq(WINDOW, KV_WIDTH)
    return pl.pallas_call(
        functools.partial(_swa_kernel, prev_from_grid=from_grid),
        grid=rows.grid,
        in_specs=[pl.BlockSpec(memory_space=pltpu.SMEM),
                  rows.act(ATT_WIDTH, col0 // ATT_WIDTH), rows.act(KV_WIDTH, kcol),
                  rows.act(KV_WIDTH, kcol + 1), kp_spec, vp_spec],
        out_specs=rows.act(ATT_WIDTH),
        out_shape=jax.ShapeDtypeStruct((rows.nseq, rows.T, ATT_WIDTH), BF16),
        compiler_params=_params("parallel", "arbitrary"),
        name=name,
    )(sinks, z, z, z, k_prev, v_prev)


def kernel(x_prompt, x_sample, c_prompt, c_sample, state_conv, state_gla, cache_k, cache_v, w_ada, b_ada, g_mix, g_ffn, w_in_ab, conv_w, conv_b, conv_ln_g, conv_ln_b, gla_w_gate, gla_b_gate, gla_g_norm, w_in_cd, sgu_w, sgu_b, sgu_ln_g, sgu_ln_b, att_sinks, w_out, w_ffn_gate, w_ffn_up, w_ffn_down, g_final):
    nb, seq, d = x_prompt.shape
    ndb, dseq, _ = x_sample.shape
    ns = ndb * dseq
    assert d == D_MODEL and dseq <= SAMPLE_ROWS and seq % 512 == 0 and ns % 256 == 0
    sb = ROW_BLOCK // SAMPLE_ROWS
    assert ndb % sb == 0

    w_ab = jnp.pad(w_in_ab, ((0, 0), (0, IN_AB_PAD - IN_AB))).astype(BF16)
    w_cd = w_in_cd.astype(BF16)
    w_o = w_out.astype(BF16)
    w_gu, w_fd = _ffn_weights(w_ffn_gate, w_ffn_up, w_ffn_down, FFN_CHUNK)

    tri = jnp.tril(jnp.ones((ROW_BLOCK, ROW_BLOCK), bool))
    sgu_wp = jnp.where(tri, sgu_w, 0.0).astype(BF16)
    sgu_ws = jnp.where(tri, sgu_w, 0.0)[:, :SAMPLE_ROWS, :SAMPLE_ROWS]
    sgu_ws = jnp.einsum("ab,gts->gatbs", jnp.eye(sb, dtype=F32), sgu_ws).reshape(
        SGU_GROUPS, ROW_BLOCK, ROW_BLOCK).astype(BF16)
    sgu_bias_p = jnp.repeat(sgu_b.T, SGU_GD, axis=1)
    sgu_bias_s = jnp.tile(sgu_bias_p[:SAMPLE_ROWS], (sb, 1))

    c_all = jnp.concatenate([jnp.repeat(c_sample, dseq, axis=0), c_prompt], axis=0)
    mods_rows, mods_p = _ada(c_all, ns, w_ada, b_ada)
    mods_s = mods_rows.reshape(mods_rows.shape[0], 1, ns, N_MODS * D_MODEL)

    rp_dense = _Rows(nb, seq, 1, 512)
    rs_dense = _Rows(1, ns, 1, 256, per_row=True)
    rp_in = _Rows(nb, seq, 1, 256)
    rs_in = _Rows(1, ns, 1, 256, per_row=True)
    rp_mix = _Rows(nb, seq, 1, 256)
    rp_att = _Rows(nb, seq, 1, WINDOW)
    rs_mix = _Rows(ndb, SAMPLE_ROWS, sb, SAMPLE_ROWS)

    def to_mixer(z):
        z = z.reshape(ndb, dseq, z.shape[-1])
        return jnp.pad(z, ((0, 0), (0, SAMPLE_ROWS - dseq), (0, 0)))

    def to_dense(o):
        return o[:, :dseq].reshape(1, ns, o.shape[-1])

    yp, ys = x_prompt, x_sample.reshape(1, ns, d)

    conv0 = jnp.zeros((nb, CONV_HIST, D_CONV), F32)
    zp, oa_p, st_conv_p = _inproj_conv(rp_in, yp, mods_p, 0, g_mix, w_ab, conv0, conv_w, conv_b,
                                       conv_ln_g, conv_ln_b, "inproj_conv_p")
    zs = to_mixer(_inproj(rs_in, ys, mods_s, 0, g_mix, w_ab, "inproj_ab_s"))
    oa_s, st_conv_s = _conv(rs_mix, zs, state_conv, conv_w, conv_b, conv_ln_g, conv_ln_b, dseq, "conv_s")
    ob_p, st_gla_p = _gla(rp_mix, zp, 0, None, gla_w_gate, gla_b_gate, gla_g_norm, ROW_BLOCK, "gla_p")
    ob_s, st_gla_s = _gla(rs_mix, zs, 2 * D_CONV, state_gla, gla_w_gate, gla_b_gate, gla_g_norm, dseq, "gla_s")
    yp = _outproj(rp_dense, oa_p, ob_p, yp, mods_p, 0, w_o, "outproj0_p")
    ys = _outproj(rs_dense, to_dense(oa_s), to_dense(ob_s), ys, mods_s, 0, w_o, "outproj0_s")
    yp = _ffn(rp_dense, yp, mods_p, 0, g_ffn, w_gu, w_fd, g_final, False, "ffn0_p")
    ys = _ffn(rs_dense, ys, mods_s, 0, g_ffn, w_gu, w_fd, g_final, False, "ffn0_s")

    zp, oc_p = _inproj_sgu(rp_in, yp, mods_p, 1, g_mix, w_cd, sgu_wp, sgu_bias_p, sgu_ln_g, sgu_ln_b,
                           "inproj_sgu_p")
    zs = to_mixer(_inproj(rs_in, ys, mods_s, 1, g_mix, w_cd, "inproj_cd_s"))
    oc_s, sgu_v = _sgu(rs_mix, zs, sgu_ws, sgu_bias_s, sgu_ln_g, sgu_ln_b, True, "sgu_s")
    wb = cache_k.shape[1]
    od_p = _swa(rp_att, zp, 0, att_sinks, None, None, "swa_p")
    od_s = _swa(rs_mix, zs, 2 * D_SGU, att_sinks, cache_k.reshape(ndb, wb, KV_WIDTH),
                cache_v.reshape(ndb, wb, KV_WIDTH), "swa_s")
    yp = _outproj(rp_dense, oc_p, od_p, yp, mods_p, 1, w_o, "outproj1_p")
    ys = _outproj(rs_dense, to_dense(oc_s), to_dense(od_s), ys, mods_s, 1, w_o, "outproj1_s")
    yp = _ffn(rp_dense, yp, mods_p, 1, g_ffn, w_gu, w_fd, g_final, True, "ffn1_p")
    ys = _ffn(rs_dense, ys, mods_s, 1, g_ffn, w_gu, w_fd, g_final, True, "ffn1_s")

    k0 = 2 * D_SGU + ATT_WIDTH
    kv_shape = (ATT_KV_HEADS, ATT_HD)
    kp0 = ATT_WIDTH
    k_new_p = zp[:, seq - wb:, kp0:kp0 + KV_WIDTH].reshape(nb, wb, *kv_shape)
    v_new_p = zp[:, seq - wb:, kp0 + KV_WIDTH:kp0 + 2 * KV_WIDTH].reshape(nb, wb, *kv_shape)
    k_new_s = zs[:, :dseq, k0:k0 + KV_WIDTH].reshape(ndb, dseq, *kv_shape)
    v_new_s = zs[:, :dseq, k0 + KV_WIDTH:k0 + 2 * KV_WIDTH].reshape(ndb, dseq, *kv_shape)
    k_s = jnp.concatenate([cache_k, k_new_s], axis=1)[:, -wb:]
    v_s = jnp.concatenate([cache_v, v_new_s], axis=1)[:, -wb:]
    return (yp, ys.reshape(ndb, dseq, d), st_conv_p, st_conv_s, st_gla_p, st_gla_s, sgu_v[:, :dseq],
            k_new_p, k_s, v_new_p, v_s)
```

```python
import functools

import jax
import jax.numpy as jnp
from jax import lax
from jax.experimental import pallas as pl
from jax.experimental.pallas import tpu as pltpu

F32 = jnp.float32
BF16 = jnp.bfloat16

D_MODEL = 2048
NORM_EPS = 1e-6
NEG_INF = -1e30
D_CONV = 1024
CONV_WIDTH = 31
CONV_HIST = CONV_WIDTH - 1
GLA_HEADS = 4
GLA_DV = 1024
GLA_DK = 512
GLA_HK = 128
GLA_HV = 256
GLA_RANK = 16
GLA_GATE_NORM = 16.0
D_SGU = 1024
SGU_GROUPS = 4
SGU_GD = 256
ATT_HD = 64
ATT_HEADS = 16
ATT_KV_HEADS = 2
ATT_REP = 8
ATT_WIDTH = 1024
KV_WIDTH = 128
WINDOW = 128
D_FF = 5632
IN_AB = 5136
IN_AB_PAD = 5248
IN_CD = 3328
N_MODS = 6

LANES = 128
SUBLANES = 8
SAMPLE_ROWS = SUBLANES
ROW_BLOCK = 128
FFN_CHUNK = 512
FFN_ROWS = 1024
FFN_VMEM_LIMIT = 63 * 1024 * 1024
VMEM_LIMIT = 56 * 1024 * 1024

ALIBI_SLOPES = tuple(2.0 ** (-8.0 * (h + 1) / ATT_HEADS) for h in range(ATT_HEADS))


def _mm(a, b):
    return jnp.dot(a, b, preferred_element_type=F32)


def _mm_nt(a, b):
    return lax.dot_general(a, b, (((1,), (1,)), ((), ())), preferred_element_type=F32)


def _sigmoid(x):
    return 1.0 / (1.0 + jnp.exp(-x))


def _silu(x):
    return x * _sigmoid(x)


def _gelu_tanh(x):
    c = 0.7978845608028654
    return 0.5 * x * (1.0 + jnp.tanh(c * (x + 0.044715 * (x * x * x))))


def _log_sigmoid(x):
    return jnp.minimum(x, 0.0) - jnp.log(1.0 + jnp.exp(-jnp.abs(x)))


def _rms(x):
    return x * lax.rsqrt(jnp.mean(x * x, axis=-1, keepdims=True) + NORM_EPS)


def _layer_norm(x, g, b):
    mu = jnp.mean(x, axis=-1, keepdims=True)
    xc = x - mu
    return xc * lax.rsqrt(jnp.mean(xc * xc, axis=-1, keepdims=True) + NORM_EPS) * g + b


def _params(*sem):
    return pltpu.CompilerParams(dimension_semantics=sem, vmem_limit_bytes=VMEM_LIMIT)


def _const(shape):
    nd = len(shape)
    return pl.BlockSpec(shape, lambda *_: (0,) * nd)


def _layer_block(layer, tail, resident=False):
    nz = (0,) * len(tail)
    mode = dict(pipeline_mode=pl.Buffered(1)) if resident else {}
    return pl.BlockSpec((None,) + tuple(tail), lambda *_: (layer,) + nz, **mode)


def _ada_kernel(c_ref, w_ref, b_ref, rows_ref, seq_ref, a_scr):
    @pl.when((pl.program_id(0) == 0) & (pl.program_id(1) == 0))
    def _():
        a_scr[...] = _silu(c_ref[...]).astype(BF16)

    res = _mm(a_scr[...], w_ref[0].astype(BF16)) + b_ref[0]
    ns = rows_ref.shape[1]
    rows_ref[0] = res[:ns]
    for r in range(seq_ref.shape[1]):
        seq_ref[0, r] = res[ns + r:ns + r + 1]


def _ada(c_all, n_rows, w_ada, b_ada, tn=1024):
    depth, d, n = w_ada.shape
    rows = c_all.shape[0]
    nb = rows - n_rows
    return pl.pallas_call(
        _ada_kernel,
        grid=(depth, n // tn),
        in_specs=[pl.BlockSpec((rows, d), lambda l, j: (0, 0)),
                  pl.BlockSpec((1, d, tn), lambda l, j: (l, 0, j)),
                  pl.BlockSpec((1, 1, tn), lambda l, j: (l, 0, j))],
        out_specs=[pl.BlockSpec((1, n_rows, tn), lambda l, j: (l, 0, j)),
                   pl.BlockSpec((1, nb, 1, tn), lambda l, j: (l, 0, 0, j))],
        out_shape=[jax.ShapeDtypeStruct((depth, n_rows, n), F32),
                   jax.ShapeDtypeStruct((depth, nb, 1, n), F32)],
        scratch_shapes=[pltpu.VMEM((rows, d), BF16)],
        compiler_params=_params("arbitrary", "arbitrary"),
        name="ada",
    )(c_all, w_ada, b_ada.reshape(depth, 1, n))


class _Rows:
    def __init__(self, nseq, T, bb, tT, per_row=False):
        assert nseq % bb == 0 and T % tT == 0 and (nseq == 1 or not per_row)
        self.nseq, self.T, self.bb, self.tT = nseq, T, bb, tT
        self.nt = T // tT
        self.grid = (nseq // bb, self.nt)
        self.ntiles = self.grid[0] * self.nt
        self.per_row = per_row

    def _tile(self, tile):
        nt = self.nt
        return tile if tile is not None else (lambda i, t, *_: i * nt + t)

    def act(self, width, col=0, tile=None):
        nt, fn = self.nt, self._tile(tile)
        return pl.BlockSpec((self.bb, self.tT, width), lambda *g: (fn(*g) // nt, fn(*g) % nt, col))

    def mod(self, layer, k, tile=None):
        nt, fn = self.nt, self._tile(tile)
        if self.per_row:
            return pl.BlockSpec((None, 1, self.tT, D_MODEL), lambda *g: (layer, 0, fn(*g) % nt, k))
        return pl.BlockSpec((None, self.bb, 1, D_MODEL), lambda *g: (layer, fn(*g) // nt, 0, k))

    def seq(self, *tail):
        nz = (0,) * len(tail)
        return pl.BlockSpec((self.bb,) + tuple(tail), lambda i, t, *_: (i,) + nz)


def _modulated(x, g, scale, shift):
    return (_rms(x) * g) * (1.0 + scale) + shift


def _inproj_kernel(x_ref, sh_ref, sc_ref, g_ref, w_ref, z_ref, *, col_chunk):
    bb, tT, d = x_ref.shape
    h = _modulated(x_ref[...], g_ref[...], sc_ref[...], sh_ref[...])
    h = h.reshape(bb * tT, d).astype(BF16)
    n = w_ref.shape[1]
    for c0 in range(0, n, col_chunk):
        c1 = min(n, c0 + col_chunk)
        z_ref[:, :, c0:c1] = _mm(h, w_ref[:, c0:c1]).reshape(bb, tT, c1 - c0)


def _inproj(rows, x, mods, layer, g, w, name):
    n = w.shape[1]
    return pl.pallas_call(
        functools.partial(_inproj_kernel, col_chunk=512),
        grid=rows.grid,
        in_specs=[rows.act(D_MODEL), rows.mod(layer, 0), rows.mod(layer, 1),
                  _layer_block(layer, (1, 1, D_MODEL)),
                  pl.BlockSpec((D_MODEL, n), lambda *_: (0, 0), pipeline_mode=pl.Buffered(1))],
        out_specs=rows.act(n),
        out_shape=jax.ShapeDtypeStruct((rows.nseq, rows.T, n), F32),
        compiler_params=_params("parallel", "parallel"),
        name=name,
    )(x, mods, mods, g.reshape(-1, 1, 1, D_MODEL), w)


def _outproj_kernel(oa_ref, ob_ref, y_ref, gate_ref, w_ref, o_ref):
    bb, tT, d = y_ref.shape
    half = oa_ref.shape[-1]
    oa = oa_ref[...].reshape(bb * tT, half)
    ob = ob_ref[...].reshape(bb * tT, half)
    acc = _mm(oa, w_ref[:half, :]) + _mm(ob, w_ref[half:, :])
    o_ref[...] = y_ref[...] + gate_ref[...] * acc.reshape(bb, tT, d)


def _outproj(rows, oa, ob, y, mods, layer, w, name):
    half = oa.shape[-1]
    return pl.pallas_call(
        _outproj_kernel,
        grid=rows.grid,
        in_specs=[rows.act(half), rows.act(half), rows.act(D_MODEL), rows.mod(layer, 2),
                  _layer_block(layer, (D_MODEL, D_MODEL), resident=True)],
        out_specs=rows.act(D_MODEL),
        out_shape=jax.ShapeDtypeStruct(y.shape, F32),
        compiler_params=_params("parallel", "parallel"),
        name=name,
    )(oa, ob, y, mods, w)


def _ffn_kernel(y_ref, sh_ref, sc_ref, gt_ref, g_ref, wg_ref, wu_ref, wd_ref, gf_ref, o_ref,
                h_scr, *, final):
    f = pl.program_id(2)
    bb, tT, d = y_ref.shape

    @pl.when(f == 0)
    def _():
        h = _modulated(y_ref[...], g_ref[...], sc_ref[...], sh_ref[...])
        h_scr[...] = h.reshape(bb * tT, d).astype(BF16)
        o_ref[...] = jnp.zeros_like(o_ref)

    h = h_scr[...]
    a = _mm(h, wg_ref[...])
    b = _mm(h, wu_ref[...])
    p = (_silu(a) * b).astype(BF16)
    o_ref[...] += _mm(p, wd_ref[...]).reshape(bb, tT, d)

    @pl.when(f == pl.num_programs(2) - 1)
    def _():
        out = y_ref[...] + gt_ref[...] * o_ref[...]
        if final:
            out = _rms(out) * gf_ref[...]
        o_ref[...] = out


def _ffn(rows, y, mods, layer, g, wg, wu, wd, g_final, final, name, tf=FFN_CHUNK):
    nf = D_FF // tf
    m = rows.bb * rows.tT
    return pl.pallas_call(
        functools.partial(_ffn_kernel, final=final),
        grid=rows.grid + (nf,),
        in_specs=[rows.act(D_MODEL), rows.mod(layer, 3), rows.mod(layer, 4), rows.mod(layer, 5),
                  _layer_block(layer, (1, 1, D_MODEL)),
                  pl.BlockSpec((None, D_MODEL, tf), lambda i, t, f: (layer, 0, f)),
                  pl.BlockSpec((None, D_MODEL, tf), lambda i, t, f: (layer, 0, f)),
                  pl.BlockSpec((None, tf, D_MODEL), lambda i, t, f: (layer, f, 0)),
                  _const((1, 1, D_MODEL))],
        out_specs=rows.act(D_MODEL),
        out_shape=jax.ShapeDtypeStruct(y.shape, F32),
        scratch_shapes=[pltpu.VMEM((m, D_MODEL), BF16)],
        compiler_params=pltpu.CompilerParams(dimension_semantics=("parallel", "parallel", "arbitrary"),
                                             vmem_limit_bytes=FFN_VMEM_LIMIT),
        name=name,
    )(y, mods, mods, mods, g.reshape(-1, 1, 1, D_MODEL), wg, wu, wd, g_final.reshape(1, 1, D_MODEL))


CONV_PAD = 32


def _conv_kernel(za_ref, prev_ref, w_ref, cb_ref, g_ref, b_ref, y_ref, st_ref, ubuf, ybuf,
                 *, t_valid, rc, lc):
    t = pl.program_id(1)
    bb, tT, _ = za_ref.shape
    lo = CONV_PAD - CONV_HIST
    assert lc == LANES
    nlc = D_CONV // lc

    @pl.when(t == 0)
    def _():
        for b in range(bb):
            for ci in range(nlc):
                ubuf[b * nlc + ci, lo:CONV_PAD, :] = prev_ref[b, :, ci * lc:(ci + 1) * lc]

    @pl.when(t > 0)
    def _():
        ubuf[:, lo:CONV_PAD, :] = ubuf[:, tT + lo:tT + CONV_PAD, :]

    for b in range(bb):
        for ci in range(nlc):
            c0 = ci * lc
            ubuf[b * nlc + ci, CONV_PAD:CONV_PAD + tT, :] = (
                za_ref[b, :, c0:c0 + lc] * _sigmoid(za_ref[b, :, D_CONV + c0:D_CONV + c0 + lc]))

    for b in range(bb):
        def row_chunk(i, carry, b=b):
            r0 = pl.multiple_of(i * rc, SUBLANES)
            for ci in range(nlc):
                c0 = ci * lc
                acc = jnp.broadcast_to(cb_ref[:, c0:c0 + lc], (rc, lc))
                for j in range(CONV_WIDTH):
                    acc = acc + w_ref[j:j + 1, c0:c0 + lc] * ubuf[b * nlc + ci, pl.ds(r0 + lo + j, rc), :]
                ybuf[b, pl.ds(r0, rc), c0:c0 + lc] = acc
            return carry
        lax.fori_loop(0, tT // rc, row_chunk, 0)

    y = _layer_norm(ybuf[...], g_ref[...], b_ref[...])
    y_ref[...] = _silu(y).astype(y_ref.dtype)

    @pl.when(t == pl.num_programs(1) - 1)
    def _():
        for b in range(bb):
            for ci in range(nlc):
                st_ref[b, :, ci * lc:(ci + 1) * lc] = ubuf[b * nlc + ci, t_valid + lo:t_valid + CONV_PAD, :]


def _conv(rows, z, prev, conv_w, conv_b, ln_g, ln_b, t_valid, name):
    bb, tT = rows.bb, rows.tT
    rc = min(tT, 64)
    return pl.pallas_call(
        functools.partial(_conv_kernel, t_valid=t_valid, rc=rc, lc=LANES),
        grid=rows.grid,
        in_specs=[rows.act(2 * D_CONV, 0), rows.seq(CONV_HIST, D_CONV),
                  _const((CONV_WIDTH, D_CONV)), _const((1, D_CONV)), _const((1, D_CONV)), _const((1, D_CONV))],
        out_specs=[rows.act(D_CONV), rows.seq(CONV_HIST, D_CONV)],
        out_shape=[jax.ShapeDtypeStruct((rows.nseq, rows.T, D_CONV), BF16),
                   jax.ShapeDtypeStruct((rows.nseq, CONV_HIST, D_CONV), F32)],
        scratch_shapes=[pltpu.VMEM((bb * (D_CONV // LANES), CONV_PAD + tT, LANES), F32),
                        pltpu.VMEM((bb, tT, D_CONV), F32)],
        compiler_params=_params("parallel", "arbitrary"),
        name=name,
    )(z, prev, conv_w, conv_b.reshape(1, -1), ln_g.reshape(1, -1), ln_b.reshape(1, -1))


def _inproj_conv_kernel(x_ref, sh_ref, sc_ref, g_ref, w_ref, prev_ref, cw_ref, cb_ref, lg_ref, lb_ref,
                        z_ref, y_ref, st_ref, ubuf, ybuf, *, rc, col_chunk):
    t = pl.program_id(1)
    _, tT, d = x_ref.shape
    lo = CONV_PAD - CONV_HIST
    lc = LANES
    nlc = D_CONV // lc
    n_rest = z_ref.shape[-1]

    @pl.when(t == 0)
    def _():
        for ci in range(nlc):
            ubuf[ci, lo:CONV_PAD, :] = prev_ref[0, :, ci * lc:(ci + 1) * lc]

    @pl.when(t > 0)
    def _():
        ubuf[:, lo:CONV_PAD, :] = ubuf[:, tT + lo:tT + CONV_PAD, :]

    h = _modulated(x_ref[...], g_ref[...], sc_ref[...], sh_ref[...]).reshape(tT, d).astype(BF16)

    for c0 in range(0, D_CONV, col_chunk):
        u = _mm(h, w_ref[:, c0:c0 + col_chunk]) * _sigmoid(_mm(h, w_ref[:, D_CONV + c0:D_CONV + c0 + col_chunk]))
        for k in range(col_chunk // lc):
            ubuf[c0 // lc + k, CONV_PAD:CONV_PAD + tT, :] = u[:, k * lc:(k + 1) * lc]

    def project_rest(c0):
        c1 = min(n_rest, c0 + col_chunk)
        z_ref[0, :, c0:c1] = _mm(h, w_ref[:, 2 * D_CONV + c0:2 * D_CONV + c1])

    def conv_rows(r0):
        for ci in range(nlc):
            c0 = ci * lc
            acc = jnp.broadcast_to(cb_ref[:, c0:c0 + lc], (rc, lc))
            for j in range(CONV_WIDTH):
                acc = acc + cw_ref[j:j + 1, c0:c0 + lc] * ubuf[ci, r0 + lo + j:r0 + lo + j + rc, :]
            ybuf[r0:r0 + rc, c0:c0 + lc] = acc
        y = _layer_norm(ybuf[r0:r0 + rc, :], lg_ref[...], lb_ref[...])
        y_ref[0, r0:r0 + rc, :] = _silu(y).astype(y_ref.dtype)

    rest = list(range(0, n_rest, col_chunk))
    conv = list(range(0, tT, rc))
    while rest or conv:
        if rest:
            project_rest(rest.pop(0))
        if conv:
            conv_rows(conv.pop(0))

    @pl.when(t == pl.num_programs(1) - 1)
    def _():
        for ci in range(nlc):
            st_ref[0, :, ci * lc:(ci + 1) * lc] = ubuf[ci, tT + lo:tT + CONV_PAD, :]


def _inproj_conv(rows, x, mods, layer, g, w, prev, conv_w, conv_b, ln_g, ln_b, name):
    assert rows.bb == 1
    tT = rows.tT
    n_rest = w.shape[1] - 2 * D_CONV
    return pl.pallas_call(
        functools.partial(_inproj_conv_kernel, rc=32, col_chunk=512),
        grid=rows.grid,
        in_specs=[rows.act(D_MODEL), rows.mod(layer, 0), rows.mod(layer, 1),
                  _layer_block(layer, (1, 1, D_MODEL)),
                  pl.BlockSpec(w.shape, lambda *_: (0, 0), pipeline_mode=pl.Buffered(1)),
                  rows.seq(CONV_HIST, D_CONV),
                  _const((CONV_WIDTH, D_CONV)), _const((1, D_CONV)), _const((1, D_CONV)), _const((1, D_CONV))],
        out_specs=[rows.act(n_rest), rows.act(D_CONV), rows.seq(CONV_HIST, D_CONV)],
        out_shape=[jax.ShapeDtypeStruct((rows.nseq, rows.T, n_rest), F32),
                   jax.ShapeDtypeStruct((rows.nseq, rows.T, D_CONV), BF16),
                   jax.ShapeDtypeStruct((rows.nseq, CONV_HIST, D_CONV), F32)],
        scratch_shapes=[pltpu.VMEM((D_CONV // LANES, CONV_PAD + tT, LANES), F32),
                        pltpu.VMEM((tT, D_CONV), F32)],
        compiler_params=_params("parallel", "arbitrary"),
        name=name,
    )(x, mods, mods, g.reshape(-1, 1, 1, D_MODEL), w, prev, conv_w, conv_b.reshape(1, -1),
      ln_g.reshape(1, -1), ln_b.reshape(1, -1))


def _segment_cumsum(x, seg):
    pos = lax.broadcasted_iota(jnp.int32, x.shape, 0) % seg
    s = 1
    while s < seg:
        x = x + jnp.where(pos >= s, pltpu.roll(x, s, 0), 0.0)
        s *= 2
    return x


def _gla_kernel(*refs, nseg, seg, t_valid, has_state):
    if has_state:
        (zq_ref, zk_ref, zv_ref, zg_ref, zr_ref, wg_ref, bg_ref, gn_ref, s0_ref,
         y_ref, st_ref, st_scr) = refs
    else:
        (zq_ref, zk_ref, zv_ref, zg_ref, zr_ref, wg_ref, bg_ref, gn_ref,
         y_ref, st_ref, st_scr) = refs
    t = pl.program_id(1)
    bb, tT, _ = zq_ref.shape
    R = ROW_BLOCK
    assert nseg * seg == R and (bb * tT) % R == 0

    @pl.when(t == 0)
    def _():
        for g in range(nseg):
            for h in range(GLA_HEADS):
                st_scr[g, h] = s0_ref[g, h].T if has_state else jnp.zeros((GLA_HV, GLA_HK), F32)

    def rows_of(ref, c):
        if nseg == 1:
            return ref[0, c * R:(c + 1) * R, :]
        return ref[...].reshape(R, ref.shape[-1])

    row = lax.broadcasted_iota(jnp.int32, (R, R), 0)
    col = lax.broadcasted_iota(jnp.int32, (R, R), 1)
    causal = (row >= col) & ((row // seg) == (col // seg))
    pos = lax.broadcasted_iota(jnp.int32, (R, GLA_DK), 0) % seg
    colseg = lax.broadcasted_iota(jnp.int32, (GLA_HV, R), 1) // seg

    for c in range(bb * tT // R):
        zr = rows_of(zr_ref, c).astype(BF16)
        la = _log_sigmoid(_mm(zr, wg_ref[...]) + bg_ref[...]) * (1.0 / GLA_GATE_NORM)
        q = rows_of(zq_ref, c) * (GLA_HK ** -0.5)
        k = rows_of(zk_ref, c)
        v = rows_of(zv_ref, c)
        zg = rows_of(zg_ref, c)
        if t_valid < seg:
            la = jnp.where(pos < t_valid, la, 0.0)
            k = jnp.where(pos < t_valid, k, 0.0)
        bcum = _segment_cumsum(la, seg)
        b3 = bcum.reshape(nseg, seg, GLA_DK)
        bl3 = b3[:, seg - 1:seg, :]
        k_last = k * jnp.exp(bl3 - b3).reshape(R, GLA_DK)
        dec3 = jnp.exp(bl3)
        q_dec = q * jnp.exp(bcum)
        bmid = b3[:, seg // 2 - 1:seg // 2, :] if nseg == 1 else jnp.zeros_like(bl3)
        q_in = (q * jnp.exp(b3 - bmid).reshape(R, GLA_DK)).astype(BF16)
        k_in = (k * jnp.exp(bmid - b3).reshape(R, GLA_DK)).astype(BF16)
        q_dec = q_dec.astype(BF16)
        k_last = k_last.astype(BF16)
        for h in range(GLA_HEADS):
            ks = slice(h * GLA_HK, (h + 1) * GLA_HK)
            vs = slice(h * GLA_HV, (h + 1) * GLA_HV)
            v_h = v[:, vs]
            att = jnp.where(causal, _mm_nt(q_in[:, ks], k_in[:, ks]), 0.0)
            o = _mm(att.astype(BF16), v_h.astype(BF16))
            vt = v_h.T
            inter = []
            for g in range(nseg):
                st = st_scr[g, h]
                inter.append(_mm_nt(q_dec[g * seg:(g + 1) * seg, ks], st.astype(BF16)))
                vt_g = vt if nseg == 1 else jnp.where(colseg == g, vt, 0.0)
                st_scr[g, h] = st * dec3[g, :, ks] + _mm(vt_g.astype(BF16), k_last[:, ks])
            o = o + (inter[0] if nseg == 1 else jnp.concatenate(inter, axis=0))
            o = _rms(o) * gn_ref[...] * _silu(zg[:, vs])
            o = o.astype(y_ref.dtype)
            if nseg == 1:
                y_ref[0, c * R:(c + 1) * R, vs] = o
            else:
                y_ref[:, :, vs] = o.reshape(bb, tT, GLA_HV)

    @pl.when(t == pl.num_programs(1) - 1)
    def _():
        for g in range(nseg):
            for h in range(GLA_HEADS):
                st_ref[g, h] = st_scr[g, h].T


def _gla(rows, z, col0, s0, w_gate, b_gate, g_norm, t_valid, name):
    bb = rows.bb
    nseg = bb if bb > 1 else 1
    seg = ROW_BLOCK // nseg
    has_state = s0 is not None
    wg = jnp.pad(w_gate, ((0, LANES - GLA_RANK), (0, 0))).astype(BF16)
    state_spec = rows.seq(GLA_HEADS, GLA_HK, GLA_HV)
    in_specs = [rows.act(GLA_DK, col0 // GLA_DK), rows.act(GLA_DK, col0 // GLA_DK + 1),
                rows.act(GLA_DV, (col0 + 2 * GLA_DK) // GLA_DV),
                rows.act(GLA_DV, (col0 + 2 * GLA_DK) // GLA_DV + 1),
                rows.act(LANES, (col0 + 2 * GLA_DK + 2 * GLA_DV) // LANES),
                _const((LANES, GLA_DK)), _const((1, GLA_DK)), _const((1, GLA_HV))]
    args = [z, z, z, z, z, wg, b_gate.reshape(1, -1), g_norm.reshape(1, -1)]
    if has_state:
        in_specs.append(state_spec)
        args.append(s0)
    return pl.pallas_call(
        functools.partial(_gla_kernel, nseg=nseg, seg=seg, t_valid=t_valid, has_state=has_state),
        grid=rows.grid,
        in_specs=in_specs,
        out_specs=[rows.act(GLA_DV), state_spec],
        out_shape=[jax.ShapeDtypeStruct((rows.nseq, rows.T, GLA_DV), BF16),
                   jax.ShapeDtypeStruct((rows.nseq, GLA_HEADS, GLA_HK, GLA_HV), F32)],
        scratch_shapes=[pltpu.VMEM((nseg, GLA_HEADS, GLA_HV, GLA_HK), F32)],
        compiler_params=_params("parallel", "arbitrary"),
        name=name,
    )(*args)


def _sgu_kernel(zu_ref, zv_ref, w_ref, bias_ref, g_ref, b_ref, y_ref, *maybe_v_ref):
    bb, tT, _ = zu_ref.shape
    R = ROW_BLOCK
    for c in range(bb * tT // R):
        if bb == 1:
            zu = zu_ref[0, c * R:(c + 1) * R, :]
            zv = zv_ref[0, c * R:(c + 1) * R, :]
        else:
            zu = zu_ref[...].reshape(R, D_SGU)
            zv = zv_ref[...].reshape(R, D_SGU)
        v = _layer_norm(_gelu_tanh(zv), g_ref[...], b_ref[...])
        vb = v.astype(BF16)
        mixed = jnp.concatenate(
            [_mm(w_ref[g], vb[:, g * SGU_GD:(g + 1) * SGU_GD]) for g in range(SGU_GROUPS)], axis=1)
        out = (_gelu_tanh(zu) * (mixed + bias_ref[...])).astype(y_ref.dtype)
        if bb == 1:
            y_ref[0, c * R:(c + 1) * R, :] = out
        else:
            y_ref[...] = out.reshape(bb, tT, D_SGU)
        for v_ref in maybe_v_ref:
            if bb == 1:
                v_ref[0, c * R:(c + 1) * R, :] = v
            else:
                v_ref[...] = v.reshape(bb, tT, D_SGU)


def _sgu(rows, z, w_blocks, bias_rows, ln_g, ln_b, want_v, name):
    out_specs = [rows.act(D_SGU)]
    out_shape = [jax.ShapeDtypeStruct((rows.nseq, rows.T, D_SGU), BF16)]
    if want_v:
        out_specs.append(rows.act(D_SGU))
        out_shape.append(jax.ShapeDtypeStruct((rows.nseq, rows.T, D_SGU), F32))
    return pl.pallas_call(
        _sgu_kernel,
        grid=rows.grid,
        in_specs=[rows.act(D_SGU, 0), rows.act(D_SGU, 1),
                  _const((SGU_GROUPS, ROW_BLOCK, ROW_BLOCK)), _const((ROW_BLOCK, D_SGU)),
                  _const((1, D_SGU)), _const((1, D_SGU))],
        out_specs=out_specs,
        out_shape=out_shape,
        compiler_params=_params("parallel", "parallel"),
        name=name,
    )(z, z, w_blocks, bias_rows, ln_g.reshape(1, -1), ln_b.reshape(1, -1))


def _inproj_sgu_kernel(x_ref, sh_ref, sc_ref, g_ref, w_ref, ws_ref, bias_ref, lg_ref, lb_ref,
                       z_ref, y_ref, zu_scr, zv_scr, *, col_chunk):
    _, tT, d = x_ref.shape
    n_rest = z_ref.shape[-1]
    R = ROW_BLOCK
    h = _modulated(x_ref[...], g_ref[...], sc_ref[...], sh_ref[...]).reshape(tT, d).astype(BF16)
    for c0 in range(0, D_SGU, col_chunk):
        zv_scr[:, c0:c0 + col_chunk] = _mm(h, w_ref[:, D_SGU + c0:D_SGU + c0 + col_chunk])
    for c0 in range(0, D_SGU, col_chunk):
        zu_scr[:, c0:c0 + col_chunk] = _mm(h, w_ref[:, c0:c0 + col_chunk])

    def project_rest(c0):
        c1 = min(n_rest, c0 + col_chunk)
        z_ref[0, :, c0:c1] = _mm(h, w_ref[:, 2 * D_SGU + c0:2 * D_SGU + c1])

    def gate_rows(r0):
        v = _layer_norm(_gelu_tanh(zv_scr[r0:r0 + R, :]), lg_ref[...], lb_ref[...]).astype(BF16)
        mixed = jnp.concatenate(
            [_mm(ws_ref[g], v[:, g * SGU_GD:(g + 1) * SGU_GD]) for g in range(SGU_GROUPS)], axis=1)
        y_ref[0, r0:r0 + R, :] = (_gelu_tanh(zu_scr[r0:r0 + R, :]) * (mixed + bias_ref[...])).astype(y_ref.dtype)

    rest = list(range(0, n_rest, col_chunk))
    gate = list(range(0, tT, R))
    while rest or gate:
        if rest:
            project_rest(rest.pop(0))
        if gate:
            gate_rows(gate.pop(0))


def _inproj_sgu(rows, x, mods, layer, g, w, w_blocks, bias_rows, ln_g, ln_b, name):
    assert rows.bb == 1 and rows.tT % ROW_BLOCK == 0
    tT = rows.tT
    n_rest = w.shape[1] - 2 * D_SGU
    return pl.pallas_call(
        functools.partial(_inproj_sgu_kernel, col_chunk=512),
        grid=rows.grid,
        in_specs=[rows.act(D_MODEL), rows.mod(layer, 0), rows.mod(layer, 1),
                  _layer_block(layer, (1, 1, D_MODEL)),
                  pl.BlockSpec(w.shape, lambda *_: (0, 0), pipeline_mode=pl.Buffered(1)),
                  _const((SGU_GROUPS, ROW_BLOCK, ROW_BLOCK)), _const((ROW_BLOCK, D_SGU)),
                  _const((1, D_SGU)), _const((1, D_SGU))],
        out_specs=[rows.act(n_rest), rows.act(D_SGU)],
        out_shape=[jax.ShapeDtypeStruct((rows.nseq, rows.T, n_rest), F32),
                   jax.ShapeDtypeStruct((rows.nseq, rows.T, D_SGU), BF16)],
        scratch_shapes=[pltpu.VMEM((tT, D_SGU), F32), pltpu.VMEM((tT, D_SGU), F32)],
        compiler_params=_params("parallel", "parallel"),
        name=name,
    )(x, mods, mods, g.reshape(-1, 1, 1, D_MODEL), w, w_blocks, bias_rows,
      ln_g.reshape(1, -1), ln_b.reshape(1, -1))


def _swa_kernel(sink_ref, q_ref, kc_ref, vc_ref, kp_ref, vp_ref, y_ref, *, prev_from_grid):
    bb, rq, _ = q_ref.shape
    W = WINDOW
    first_key = jnp.where(pl.program_id(1) > 0, 0, W) if prev_from_grid else 0
    tq = lax.broadcasted_iota(jnp.int32, (rq, 2 * W), 0)
    sk = lax.broadcasted_iota(jnp.int32, (rq, 2 * W), 1)
    dist_i = tq + W - sk
    valid = (dist_i >= 0) & (dist_i < W) & (sk >= first_key)
    dist = dist_i.astype(F32)
    scale = ATT_HD ** -0.5
    for b in range(bb):
        q = q_ref[b]
        kc, vc = kc_ref[b], vc_ref[b]
        if rq < W:
            pad = jnp.zeros((W - rq, KV_WIDTH), F32)
            kc = jnp.concatenate([kc, pad], axis=0)
            vc = jnp.concatenate([vc, pad], axis=0)
        kp, vp = kp_ref[b], vp_ref[b]
        for g in range(ATT_KV_HEADS):
            gs = slice(g * ATT_HD, (g + 1) * ATT_HD)
            qg = jnp.concatenate(
                [q[:, (g * ATT_REP + r) * ATT_HD:(g * ATT_REP + r + 1) * ATT_HD] for r in range(ATT_REP)],
                axis=0).astype(BF16)
            s_all = jnp.concatenate([_mm_nt(qg, kp[:, gs].astype(BF16)),
                                     _mm_nt(qg, kc[:, gs].astype(BF16))], axis=1) * scale
            probs, inv = [], []
            for r in range(ATT_REP):
                h = g * ATT_REP + r
                sink = sink_ref[h]
                s = jnp.where(valid, s_all[r * rq:(r + 1) * rq, :] - ALIBI_SLOPES[h] * dist, NEG_INF)
                m = jnp.maximum(jnp.max(s, axis=-1, keepdims=True), sink)
                p = jnp.exp(s - m)
                inv.append(1.0 / (jnp.sum(p, axis=-1, keepdims=True) + jnp.exp(sink - m)))
                probs.append(p.astype(BF16))
            p_all = jnp.concatenate(probs, axis=0)
            o = _mm(p_all[:, :W], vp[:, gs].astype(BF16)) + _mm(p_all[:, W:], vc[:, gs].astype(BF16))
            for r in range(ATT_REP):
                h = g * ATT_REP + r
                y_ref[b, :, h * ATT_HD:(h + 1) * ATT_HD] = (o[r * rq:(r + 1) * rq, :] * inv[r]).astype(y_ref.dtype)


def _swa(rows, z, col0, sinks, k_prev, v_prev, name):
    bb, tT = rows.bb, rows.tT
    kcol = (col0 + ATT_WIDTH) // KV_WIDTH
    from_grid = k_prev is None
    if from_grid:
        assert tT == WINDOW and bb == 1
        kp_spec = pl.BlockSpec((1, WINDOW, KV_WIDTH), lambda i, t: (i, jnp.maximum(t - 1, 0), kcol))
        vp_spec = pl.BlockSpec((1, WINDOW, KV_WIDTH), lambda i, t: (i, jnp.maximum(t - 1, 0), kcol + 1))
        k_prev, v_prev = z, z
    else:
        kp_spec = vp_spec = rows.seq(WINDOW, KV_WIDTH)
    return pl.pallas_call(
        functools.partial(_swa_kernel, prev_from_grid=from_grid),
        grid=rows.grid,
        in_specs=[pl.BlockSpec(memory_space=pltpu.SMEM),
                  rows.act(ATT_WIDTH, col0 // ATT_WIDTH), rows.act(KV_WIDTH, kcol),
                  rows.act(KV_WIDTH, kcol + 1), kp_spec, vp_spec],
        out_specs=rows.act(ATT_WIDTH),
        out_shape=jax.ShapeDtypeStruct((rows.nseq, rows.T, ATT_WIDTH), BF16),
        compiler_params=_params("parallel", "arbitrary"),
        name=name,
    )(sinks, z, z, z, k_prev, v_prev)


def kernel(x_prompt, x_sample, c_prompt, c_sample, state_conv, state_gla, cache_k, cache_v, w_ada, b_ada, g_mix, g_ffn, w_in_ab, conv_w, conv_b, conv_ln_g, conv_ln_b, gla_w_gate, gla_b_gate, gla_g_norm, w_in_cd, sgu_w, sgu_b, sgu_ln_g, sgu_ln_b, att_sinks, w_out, w_ffn_gate, w_ffn_up, w_ffn_down, g_final):
    nb, seq, d = x_prompt.shape
    ndb, dseq, _ = x_sample.shape
    ns = ndb * dseq
    assert d == D_MODEL and dseq <= SAMPLE_ROWS and seq % 512 == 0 and ns % 256 == 0
    sb = ROW_BLOCK // SAMPLE_ROWS
    assert ndb % sb == 0

    w_ab = jnp.pad(w_in_ab, ((0, 0), (0, IN_AB_PAD - IN_AB))).astype(BF16)
    w_cd = w_in_cd.astype(BF16)
    w_o = w_out.astype(BF16)
    w_fg, w_fu, w_fd = w_ffn_gate.astype(BF16), w_ffn_up.astype(BF16), w_ffn_down.astype(BF16)

    tri = jnp.tril(jnp.ones((ROW_BLOCK, ROW_BLOCK), bool))
    sgu_wp = jnp.where(tri, sgu_w, 0.0).astype(BF16)
    sgu_ws = jnp.where(tri, sgu_w, 0.0)[:, :SAMPLE_ROWS, :SAMPLE_ROWS]
    sgu_ws = jnp.einsum("ab,gts->gatbs", jnp.eye(sb, dtype=F32), sgu_ws).reshape(
        SGU_GROUPS, ROW_BLOCK, ROW_BLOCK).astype(BF16)
    sgu_bias_p = jnp.repeat(sgu_b.T, SGU_GD, axis=1)
    sgu_bias_s = jnp.tile(sgu_bias_p[:SAMPLE_ROWS], (sb, 1))

    c_all = jnp.concatenate([jnp.repeat(c_sample, dseq, axis=0), c_prompt], axis=0)
    mods_rows, mods_p = _ada(c_all, ns, w_ada, b_ada)
    mods_s = mods_rows.reshape(mods_rows.shape[0], 1, ns, N_MODS * D_MODEL)

    rp_dense = _Rows(nb, seq, 1, 512)
    rp_ffn = _Rows(nb, seq, 1, min(seq, FFN_ROWS))
    rs_dense = _Rows(1, ns, 1, min(ns, 512), per_row=True)
    rp_in = _Rows(nb, seq, 1, 256)
    rs_in = _Rows(1, ns, 1, 256, per_row=True)
    rp_mix = _Rows(nb, seq, 1, 256)
    rp_att = _Rows(nb, seq, 1, WINDOW)
    rs_mix = _Rows(ndb, SAMPLE_ROWS, sb, SAMPLE_ROWS)

    def to_mixer(z):
        z = z.reshape(ndb, dseq, z.shape[-1])
        return jnp.pad(z, ((0, 0), (0, SAMPLE_ROWS - dseq), (0, 0)))

    def to_dense(o):
        return o[:, :dseq].reshape(1, ns, o.shape[-1])

    yp, ys = x_prompt, x_sample.reshape(1, ns, d)

    conv0 = jnp.zeros((nb, CONV_HIST, D_CONV), F32)
    zp, oa_p, st_conv_p = _inproj_conv(rp_in, yp, mods_p, 0, g_mix, w_ab, conv0, conv_w, conv_b,
                                       conv_ln_g, conv_ln_b, "inproj_conv_p")
    zs = to_mixer(_inproj(rs_in, ys, mods_s, 0, g_mix, w_ab, "inproj_ab_s"))
    oa_s, st_conv_s = _conv(rs_mix, zs, state_conv, conv_w, conv_b, conv_ln_g, conv_ln_b, dseq, "conv_s")
    ob_p, st_gla_p = _gla(rp_mix, zp, 0, None, gla_w_gate, gla_b_gate, gla_g_norm, ROW_BLOCK, "gla_p")
    ob_s, st_gla_s = _gla(rs_mix, zs, 2 * D_CONV, state_gla, gla_w_gate, gla_b_gate, gla_g_norm, dseq, "gla_s")
    yp = _outproj(rp_dense, oa_p, ob_p, yp, mods_p, 0, w_o, "outproj0_p")
    ys = _outproj(rs_dense, to_dense(oa_s), to_dense(ob_s), ys, mods_s, 0, w_o, "outproj0_s")
    yp = _ffn(rp_ffn, yp, mods_p, 0, g_ffn, w_fg, w_fu, w_fd, g_final, False, "ffn0_p")
    ys = _ffn(rs_dense, ys, mods_s, 0, g_ffn, w_fg, w_fu, w_fd, g_final, False, "ffn0_s")

    zp, oc_p = _inproj_sgu(rp_in, yp, mods_p, 1, g_mix, w_cd, sgu_wp, sgu_bias_p, sgu_ln_g, sgu_ln_b,
                           "inproj_sgu_p")
    zs = to_mixer(_inproj(rs_in, ys, mods_s, 1, g_mix, w_cd, "inproj_cd_s"))
    oc_s, sgu_v = _sgu(rs_mix, zs, sgu_ws, sgu_bias_s, sgu_ln_g, sgu_ln_b, True, "sgu_s")
    wb = cache_k.shape[1]
    od_p = _swa(rp_att, zp, 0, att_sinks, None, None, "swa_p")
    od_s = _swa(rs_mix, zs, 2 * D_SGU, att_sinks, cache_k.reshape(ndb, wb, KV_WIDTH),
                cache_v.reshape(ndb, wb, KV_WIDTH), "swa_s")
    yp = _outproj(rp_dense, oc_p, od_p, yp, mods_p, 1, w_o, "outproj1_p")
    ys = _outproj(rs_dense, to_dense(oc_s), to_dense(od_s), ys, mods_s, 1, w_o, "outproj1_s")
    yp = _ffn(rp_ffn, yp, mods_p, 1, g_ffn, w_fg, w_fu, w_fd, g_final, True, "ffn1_p")
    ys = _ffn(rs_dense, ys, mods_s, 1, g_ffn, w_fg, w_fu, w_fd, g_final, True, "ffn1_s")

    k0 = 2 * D_SGU + ATT_WIDTH
    kv_shape = (ATT_KV_HEADS, ATT_HD)
    kp0 = ATT_WIDTH
    k_new_p = zp[:, seq - wb:, kp0:kp0 + KV_WIDTH].reshape(nb, wb, *kv_shape)
    v_new_p = zp[:, seq - wb:, kp0 + KV_WIDTH:kp0 + 2 * KV_WIDTH].reshape(nb, wb, *kv_shape)
    k_new_s = zs[:, :dseq, k0:k0 + KV_WIDTH].reshape(ndb, dseq, *kv_shape)
    v_new_s = zs[:, :dseq, k0 + KV_WIDTH:k0 + 2 * KV_WIDTH].reshape(ndb, dseq, *kv_shape)
    k_s = jnp.concatenate([cache_k, k_new_s], axis=1)[:, -wb:]
    v_s = jnp.concatenate([cache_v, v_new_s], axis=1)[:, -wb:]
    return (yp, ys.reshape(ndb, dseq, d), st_conv_p, st_conv_s, st_gla_p, st_gla_s, sgu_v[:, :dseq],
            k_new_p, k_s, v_new_p, v_s)
```

```python
import functools

import jax
import jax.numpy as jnp
from jax import lax
from jax.experimental import pallas as pl
from jax.experimental.pallas import tpu as pltpu

F32 = jnp.float32
BF16 = jnp.bfloat16

D_MODEL = 2048
NORM_EPS = 1e-6
NEG_INF = -1e30
D_CONV = 1024
CONV_WIDTH = 31
CONV_HIST = CONV_WIDTH - 1
GLA_HEADS = 4
GLA_DV = 1024
GLA_DK = 512
GLA_HK = 128
GLA_HV = 256
GLA_RANK = 16
GLA_GATE_NORM = 16.0
D_SGU = 1024
SGU_GROUPS = 4
SGU_GD = 256
ATT_HD = 64
ATT_HEADS = 16
ATT_KV_HEADS = 2
ATT_REP = 8
ATT_WIDTH = 1024
KV_WIDTH = 128
WINDOW = 128
D_FF = 5632
IN_AB = 5136
IN_AB_PAD = 5248
IN_CD = 3328
N_MODS = 6

LANES = 128
SUBLANES = 8
SAMPLE_ROWS = SUBLANES
ROW_BLOCK = 128
FFN_CHUNK = 512
FFN_ROWS = 1024
FFN_VMEM_LIMIT = 63 * 1024 * 1024
VMEM_LIMIT = 56 * 1024 * 1024

ALIBI_SLOPES = tuple(2.0 ** (-8.0 * (h + 1) / ATT_HEADS) for h in range(ATT_HEADS))


def _mm(a, b):
    return jnp.dot(a, b, preferred_element_type=F32)


def _mm_nt(a, b):
    return lax.dot_general(a, b, (((1,), (1,)), ((), ())), preferred_element_type=F32)


def _sigmoid(x):
    return 1.0 / (1.0 + jnp.exp(-x))


def _silu(x):
    return x * _sigmoid(x)


def _gelu_tanh(x):
    c = 0.7978845608028654
    return 0.5 * x * (1.0 + jnp.tanh(c * (x + 0.044715 * (x * x * x))))


def _log_sigmoid(x):
    return jnp.minimum(x, 0.0) - jnp.log(1.0 + jnp.exp(-jnp.abs(x)))


def _rms(x):
    return x * lax.rsqrt(jnp.mean(x * x, axis=-1, keepdims=True) + NORM_EPS)


def _layer_norm(x, g, b):
    mu = jnp.mean(x, axis=-1, keepdims=True)
    xc = x - mu
    return xc * lax.rsqrt(jnp.mean(xc * xc, axis=-1, keepdims=True) + NORM_EPS) * g + b


def _params(*sem):
    return pltpu.CompilerParams(dimension_semantics=sem, vmem_limit_bytes=VMEM_LIMIT)


def _const(shape):
    nd = len(shape)
    return pl.BlockSpec(shape, lambda *_: (0,) * nd)


def _layer_block(layer, tail, resident=False):
    nz = (0,) * len(tail)
    mode = dict(pipeline_mode=pl.Buffered(1)) if resident else {}
    return pl.BlockSpec((None,) + tuple(tail), lambda *_: (layer,) + nz, **mode)


def _ada_kernel(c_ref, w_ref, b_ref, rows_ref, seq_ref, a_scr):
    @pl.when((pl.program_id(0) == 0) & (pl.program_id(1) == 0))
    def _():
        a_scr[...] = _silu(c_ref[...]).astype(BF16)

    res = _mm(a_scr[...], w_ref[0].astype(BF16)) + b_ref[0]
    ns = rows_ref.shape[1]
    rows_ref[0] = res[:ns]
    for r in range(seq_ref.shape[1]):
        seq_ref[0, r] = res[ns + r:ns + r + 1]


def _ada(c_all, n_rows, w_ada, b_ada, tn=1024):
    depth, d, n = w_ada.shape
    rows = c_all.shape[0]
    nb = rows - n_rows
    return pl.pallas_call(
        _ada_kernel,
        grid=(depth, n // tn),
        in_specs=[pl.BlockSpec((rows, d), lambda l, j: (0, 0)),
                  pl.BlockSpec((1, d, tn), lambda l, j: (l, 0, j)),
                  pl.BlockSpec((1, 1, tn), lambda l, j: (l, 0, j))],
        out_specs=[pl.BlockSpec((1, n_rows, tn), lambda l, j: (l, 0, j)),
                   pl.BlockSpec((1, nb, 1, tn), lambda l, j: (l, 0, 0, j))],
        out_shape=[jax.ShapeDtypeStruct((depth, n_rows, n), F32),
                   jax.ShapeDtypeStruct((depth, nb, 1, n), F32)],
        scratch_shapes=[pltpu.VMEM((rows, d), BF16)],
        compiler_params=_params("arbitrary", "arbitrary"),
        name="ada",
    )(c_all, w_ada, b_ada.reshape(depth, 1, n))


class _Rows:
    def __init__(self, nseq, T, bb, tT, per_row=False):
        assert nseq % bb == 0 and T % tT == 0 and (nseq == 1 or not per_row)
        self.nseq, self.T, self.bb, self.tT = nseq, T, bb, tT
        self.nt = T // tT
        self.grid = (nseq // bb, self.nt)
        self.ntiles = self.grid[0] * self.nt
        self.per_row = per_row

    def _tile(self, tile):
        nt = self.nt
        return tile if tile is not None else (lambda i, t, *_: i * nt + t)

    def act(self, width, col=0, tile=None):
        nt, fn = self.nt, self._tile(tile)
        return pl.BlockSpec((self.bb, self.tT, width), lambda *g: (fn(*g) // nt, fn(*g) % nt, col))

    def mod(self, layer, k, tile=None):
        nt, fn = self.nt, self._tile(tile)
        if self.per_row:
            return pl.BlockSpec((None, 1, self.tT, D_MODEL), lambda *g: (layer, 0, fn(*g) % nt, k))
        return pl.BlockSpec((None, self.bb, 1, D_MODEL), lambda *g: (layer, fn(*g) // nt, 0, k))

    def seq(self, *tail):
        nz = (0,) * len(tail)
        return pl.BlockSpec((self.bb,) + tuple(tail), lambda i, t, *_: (i,) + nz)


def _modulated(x, g, scale, shift):
    return (_rms(x) * g) * (1.0 + scale) + shift


def _inproj_kernel(x_ref, sh_ref, sc_ref, g_ref, w_ref, z_ref, *, col_chunk):
    bb, tT, d = x_ref.shape
    h = _modulated(x_ref[...], g_ref[...], sc_ref[...], sh_ref[...])
    h = h.reshape(bb * tT, d).astype(BF16)
    n = w_ref.shape[1]
    for c0 in range(0, n, col_chunk):
        c1 = min(n, c0 + col_chunk)
        z_ref[:, :, c0:c1] = _mm(h, w_ref[:, c0:c1]).reshape(bb, tT, c1 - c0)


def _inproj(rows, x, mods, layer, g, w, name):
    n = w.shape[1]
    return pl.pallas_call(
        functools.partial(_inproj_kernel, col_chunk=512),
        grid=rows.grid,
        in_specs=[rows.act(D_MODEL), rows.mod(layer, 0), rows.mod(layer, 1),
                  _layer_block(layer, (1, 1, D_MODEL)),
                  pl.BlockSpec((D_MODEL, n), lambda *_: (0, 0), pipeline_mode=pl.Buffered(1))],
        out_specs=rows.act(n),
        out_shape=jax.ShapeDtypeStruct((rows.nseq, rows.T, n), F32),
        compiler_params=_params("parallel", "parallel"),
        name=name,
    )(x, mods, mods, g.reshape(-1, 1, 1, D_MODEL), w)


def _outproj_kernel(oa_ref, ob_ref, y_ref, gate_ref, w_ref, o_ref):
    bb, tT, d = y_ref.shape
    half = oa_ref.shape[-1]
    oa = oa_ref[...].reshape(bb * tT, half)
    ob = ob_ref[...].reshape(bb * tT, half)
    acc = _mm(oa, w_ref[:half, :]) + _mm(ob, w_ref[half:, :])
    o_ref[...] = y_ref[...] + gate_ref[...] * acc.reshape(bb, tT, d)


def _outproj(rows, oa, ob, y, mods, layer, w, name):
    half = oa.shape[-1]
    return pl.pallas_call(
        _outproj_kernel,
        grid=rows.grid,
        in_specs=[rows.act(half), rows.act(half), rows.act(D_MODEL), rows.mod(layer, 2),
                  _layer_block(layer, (D_MODEL, D_MODEL), resident=True)],
        out_specs=rows.act(D_MODEL),
        out_shape=jax.ShapeDtypeStruct(y.shape, F32),
        compiler_params=_params("parallel", "parallel"),
        name=name,
    )(oa, ob, y, mods, w)


def _ffn_kernel(y_ref, sh_ref, sc_ref, gt_ref, g_ref, wg_ref, wu_ref, wd_ref, gf_ref, o_ref,
                h_scr, *, final):
    f = pl.program_id(2)
    bb, tT, d = y_ref.shape

    @pl.when(f == 0)
    def _():
        h = _modulated(y_ref[...], g_ref[...], sc_ref[...], sh_ref[...])
        h_scr[...] = h.reshape(bb * tT, d).astype(BF16)
        o_ref[...] = jnp.zeros_like(o_ref)

    h = h_scr[...]
    a = _mm(h, wg_ref[...])
    b = _mm(h, wu_ref[...])
    p = (_silu(a) * b).astype(BF16)
    o_ref[...] += _mm(p, wd_ref[...]).reshape(bb, tT, d)

    @pl.when(f == pl.num_programs(2) - 1)
    def _():
        out = y_ref[...] + gt_ref[...] * o_ref[...]
        if final:
            out = _rms(out) * gf_ref[...]
        o_ref[...] = out


def _ffn(rows, y, mods, layer, g, wg, wu, wd, g_final, final, name, tf=FFN_CHUNK):
    nf = D_FF // tf
    m = rows.bb * rows.tT
    return pl.pallas_call(
        functools.partial(_ffn_kernel, final=final),
        grid=rows.grid + (nf,),
        in_specs=[rows.act(D_MODEL), rows.mod(layer, 3), rows.mod(layer, 4), rows.mod(layer, 5),
                  _layer_block(layer, (1, 1, D_MODEL)),
                  pl.BlockSpec((None, D_MODEL, tf), lambda i, t, f: (layer, 0, f)),
                  pl.BlockSpec((None, D_MODEL, tf), lambda i, t, f: (layer, 0, f)),
                  pl.BlockSpec((None, tf, D_MODEL), lambda i, t, f: (layer, f, 0)),
                  _const((1, 1, D_MODEL))],
        out_specs=rows.act(D_MODEL),
        out_shape=jax.ShapeDtypeStruct(y.shape, F32),
        scratch_shapes=[pltpu.VMEM((m, D_MODEL), BF16)],
        compiler_params=pltpu.CompilerParams(dimension_semantics=("parallel", "parallel", "arbitrary"),
                                             vmem_limit_bytes=FFN_VMEM_LIMIT),
        name=name,
    )(y, mods, mods, mods, g.reshape(-1, 1, 1, D_MODEL), wg, wu, wd, g_final.reshape(1, 1, D_MODEL))


CONV_PAD = 32


def _conv_kernel(za_ref, prev_ref, w_ref, cb_ref, g_ref, b_ref, y_ref, st_ref, ubuf, ybuf,
                 *, t_valid, rc, lc):
    t = pl.program_id(1)
    bb, tT, _ = za_ref.shape
    lo = CONV_PAD - CONV_HIST
    assert lc == LANES
    nlc = D_CONV // lc

    @pl.when(t == 0)
    def _():
        for b in range(bb):
            for ci in range(nlc):
                ubuf[b * nlc + ci, lo:CONV_PAD, :] = prev_ref[b, :, ci * lc:(ci + 1) * lc]

    @pl.when(t > 0)
    def _():
        ubuf[:, lo:CONV_PAD, :] = ubuf[:, tT + lo:tT + CONV_PAD, :]

    for b in range(bb):
        for ci in range(nlc):
            c0 = ci * lc
            ubuf[b * nlc + ci, CONV_PAD:CONV_PAD + tT, :] = (
                za_ref[b, :, c0:c0 + lc] * _sigmoid(za_ref[b, :, D_CONV + c0:D_CONV + c0 + lc]))

    for b in range(bb):
        def row_chunk(i, carry, b=b):
            r0 = pl.multiple_of(i * rc, SUBLANES)
            for ci in range(nlc):
                c0 = ci * lc
                acc = jnp.broadcast_to(cb_ref[:, c0:c0 + lc], (rc, lc))
                for j in range(CONV_WIDTH):
                    acc = acc + w_ref[j:j + 1, c0:c0 + lc] * ubuf[b * nlc + ci, pl.ds(r0 + lo + j, rc), :]
                ybuf[b, pl.ds(r0, rc), c0:c0 + lc] = acc
            return carry
        lax.fori_loop(0, tT // rc, row_chunk, 0)

    y = _layer_norm(ybuf[...], g_ref[...], b_ref[...])
    y_ref[...] = _silu(y).astype(y_ref.dtype)

    @pl.when(t == pl.num_programs(1) - 1)
    def _():
        for b in range(bb):
            for ci in range(nlc):
                st_ref[b, :, ci * lc:(ci + 1) * lc] = ubuf[b * nlc + ci, t_valid + lo:t_valid + CONV_PAD, :]


def _conv(rows, z, prev, conv_w, conv_b, ln_g, ln_b, t_valid, name):
    bb, tT = rows.bb, rows.tT
    rc = min(tT, 64)
    return pl.pallas_call(
        functools.partial(_conv_kernel, t_valid=t_valid, rc=rc, lc=LANES),
        grid=rows.grid,
        in_specs=[rows.act(2 * D_CONV, 0), rows.seq(CONV_HIST, D_CONV),
                  _const((CONV_WIDTH, D_CONV)), _const((1, D_CONV)), _const((1, D_CONV)), _const((1, D_CONV))],
        out_specs=[rows.act(D_CONV), rows.seq(CONV_HIST, D_CONV)],
        out_shape=[jax.ShapeDtypeStruct((rows.nseq, rows.T, D_CONV), BF16),
                   jax.ShapeDtypeStruct((rows.nseq, CONV_HIST, D_CONV), F32)],
        scratch_shapes=[pltpu.VMEM((bb * (D_CONV // LANES), CONV_PAD + tT, LANES), F32),
                        pltpu.VMEM((bb, tT, D_CONV), F32)],
        compiler_params=_params("parallel", "arbitrary"),
        name=name,
    )(z, prev, conv_w, conv_b.reshape(1, -1), ln_g.reshape(1, -1), ln_b.reshape(1, -1))


def _inproj_conv_kernel(x_ref, sh_ref, sc_ref, g_ref, w_ref, prev_ref, cw_ref, cb_ref, lg_ref, lb_ref,
                        z_ref, y_ref, st_ref, ubuf, ybuf, *, rc, col_chunk):
    t = pl.program_id(1)
    _, tT, d = x_ref.shape
    lo = CONV_PAD - CONV_HIST
    lc = LANES
    nlc = D_CONV // lc
    n_rest = z_ref.shape[-1]

    @pl.when(t == 0)
    def _():
        for ci in range(nlc):
            ubuf[ci, lo:CONV_PAD, :] = prev_ref[0, :, ci * lc:(ci + 1) * lc]

    @pl.when(t > 0)
    def _():
        ubuf[:, lo:CONV_PAD, :] = ubuf[:, tT + lo:tT + CONV_PAD, :]

    h = _modulated(x_ref[...], g_ref[...], sc_ref[...], sh_ref[...]).reshape(tT, d).astype(BF16)

    for c0 in range(0, D_CONV, col_chunk):
        u = _mm(h, w_ref[:, c0:c0 + col_chunk]) * _sigmoid(_mm(h, w_ref[:, D_CONV + c0:D_CONV + c0 + col_chunk]))
        for k in range(col_chunk // lc):
            ubuf[c0 // lc + k, CONV_PAD:CONV_PAD + tT, :] = u[:, k * lc:(k + 1) * lc]

    def project_rest(c0):
        c1 = min(n_rest, c0 + col_chunk)
        z_ref[0, :, c0:c1] = _mm(h, w_ref[:, 2 * D_CONV + c0:2 * D_CONV + c1])

    def conv_rows(r0):
        for ci in range(nlc):
            c0 = ci * lc
            acc = jnp.broadcast_to(cb_ref[:, c0:c0 + lc], (rc, lc))
            for j in range(CONV_WIDTH):
                acc = acc + cw_ref[j:j + 1, c0:c0 + lc] * ubuf[ci, r0 + lo + j:r0 + lo + j + rc, :]
            ybuf[r0:r0 + rc, c0:c0 + lc] = acc
        y = _layer_norm(ybuf[r0:r0 + rc, :], lg_ref[...], lb_ref[...])
        y_ref[0, r0:r0 + rc, :] = _silu(y).astype(y_ref.dtype)

    rest = list(range(0, n_rest, col_chunk))
    conv = list(range(0, tT, rc))
    while rest or conv:
        if rest:
            project_rest(rest.pop(0))
        if conv:
            conv_rows(conv.pop(0))

    @pl.when(t == pl.num_programs(1) - 1)
    def _():
        for ci in range(nlc):
            st_ref[0, :, ci * lc:(ci + 1) * lc] = ubuf[ci, tT + lo:tT + CONV_PAD, :]


def _inproj_conv(rows, x, mods, layer, g, w, prev, conv_w, conv_b, ln_g, ln_b, name):
    assert rows.bb == 1
    tT = rows.tT
    n_rest = w.shape[1] - 2 * D_CONV
    return pl.pallas_call(
        functools.partial(_inproj_conv_kernel, rc=32, col_chunk=512),
        grid=rows.grid,
        in_specs=[rows.act(D_MODEL), rows.mod(layer, 0), rows.mod(layer, 1),
                  _layer_block(layer, (1, 1, D_MODEL)),
                  pl.BlockSpec(w.shape, lambda *_: (0, 0), pipeline_mode=pl.Buffered(1)),
                  rows.seq(CONV_HIST, D_CONV),
                  _const((CONV_WIDTH, D_CONV)), _const((1, D_CONV)), _const((1, D_CONV)), _const((1, D_CONV))],
        out_specs=[rows.act(n_rest), rows.act(D_CONV), rows.seq(CONV_HIST, D_CONV)],
        out_shape=[jax.ShapeDtypeStruct((rows.nseq, rows.T, n_rest), F32),
                   jax.ShapeDtypeStruct((rows.nseq, rows.T, D_CONV), BF16),
                   jax.ShapeDtypeStruct((rows.nseq, CONV_HIST, D_CONV), F32)],
        scratch_shapes=[pltpu.VMEM((D_CONV // LANES, CONV_PAD + tT, LANES), F32),
                        pltpu.VMEM((tT, D_CONV), F32)],
        compiler_params=_params("parallel", "arbitrary"),
        name=name,
    )(x, mods, mods, g.reshape(-1, 1, 1, D_MODEL), w, prev, conv_w, conv_b.reshape(1, -1),
      ln_g.reshape(1, -1), ln_b.reshape(1, -1))


def _segment_cumsum(x, seg):
    pos = lax.broadcasted_iota(jnp.int32, x.shape, 0) % seg
    s = 1
    while s < seg:
        x = x + jnp.where(pos >= s, pltpu.roll(x, s, 0), 0.0)
        s *= 2
    return x


def _gla_kernel(*refs, nseg, seg, t_valid, has_state):
    if has_state:
        (zq_ref, zk_ref, zv_ref, zg_ref, zr_ref, wg_ref, bg_ref, gn_ref, s0_ref,
         y_ref, st_ref, st_scr) = refs
    else:
        (zq_ref, zk_ref, zv_ref, zg_ref, zr_ref, wg_ref, bg_ref, gn_ref,
         y_ref, st_ref, st_scr) = refs
    t = pl.program_id(1)
    bb, tT, _ = zq_ref.shape
    R = ROW_BLOCK
    assert nseg * seg == R and (bb * tT) % R == 0

    @pl.when(t == 0)
    def _():
        for g in range(nseg):
            for h in range(GLA_HEADS):
                st_scr[g, h] = s0_ref[g, h].T if has_state else jnp.zeros((GLA_HV, GLA_HK), F32)

    def rows_of(ref, c):
        if nseg == 1:
            return ref[0, c * R:(c + 1) * R, :]
        return ref[...].reshape(R, ref.shape[-1])

    row = lax.broadcasted_iota(jnp.int32, (R, R), 0)
    col = lax.broadcasted_iota(jnp.int32, (R, R), 1)
    causal = (row >= col) & ((row // seg) == (col // seg))
    pos = lax.broadcasted_iota(jnp.int32, (R, GLA_DK), 0) % seg
    colseg = lax.broadcasted_iota(jnp.int32, (GLA_HV, R), 1) // seg

    for c in range(bb * tT // R):
        zr = rows_of(zr_ref, c).astype(BF16)
        la = _log_sigmoid(_mm(zr, wg_ref[...]) + bg_ref[...]) * (1.0 / GLA_GATE_NORM)
        q = rows_of(zq_ref, c) * (GLA_HK ** -0.5)
        k = rows_of(zk_ref, c)
        v = rows_of(zv_ref, c)
        zg = rows_of(zg_ref, c)
        if t_valid < seg:
            la = jnp.where(pos < t_valid, la, 0.0)
            k = jnp.where(pos < t_valid, k, 0.0)
        bcum = _segment_cumsum(la, seg)
        b3 = bcum.reshape(nseg, seg, GLA_DK)
        bl3 = b3[:, seg - 1:seg, :]
        k_last = k * jnp.exp(bl3 - b3).reshape(R, GLA_DK)
        dec3 = jnp.exp(bl3)
        q_dec = q * jnp.exp(bcum)
        bmid = b3[:, seg // 2 - 1:seg // 2, :] if nseg == 1 else jnp.zeros_like(bl3)
        q_in = (q * jnp.exp(b3 - bmid).reshape(R, GLA_DK)).astype(BF16)
        k_in = (k * jnp.exp(bmid - b3).reshape(R, GLA_DK)).astype(BF16)
        q_dec = q_dec.astype(BF16)
        k_last = k_last.astype(BF16)
        for h in range(GLA_HEADS):
            ks = slice(h * GLA_HK, (h + 1) * GLA_HK)
            vs = slice(h * GLA_HV, (h + 1) * GLA_HV)
            v_h = v[:, vs]
            att = jnp.where(causal, _mm_nt(q_in[:, ks], k_in[:, ks]), 0.0)
            o = _mm(att.astype(BF16), v_h.astype(BF16))
            vt = v_h.T
            inter = []
            for g in range(nseg):
                st = st_scr[g, h]
                inter.append(_mm_nt(q_dec[g * seg:(g + 1) * seg, ks], st.astype(BF16)))
                vt_g = vt if nseg == 1 else jnp.where(colseg == g, vt, 0.0)
                st_scr[g, h] = st * dec3[g, :, ks] + _mm(vt_g.astype(BF16), k_last[:, ks])
            o = o + (inter[0] if nseg == 1 else jnp.concatenate(inter, axis=0))
            o = _rms(o) * gn_ref[...] * _silu(zg[:, vs])
            o = o.astype(y_ref.dtype)
            if nseg == 1:
                y_ref[0, c * R:(c + 1) * R, vs] = o
            else:
                y_ref[:, :, vs] = o.reshape(bb, tT, GLA_HV)

    @pl.when(t == pl.num_programs(1) - 1)
    def _():
        for g in range(nseg):
            for h in range(GLA_HEADS):
                st_ref[g, h] = st_scr[g, h].T


def _gla(rows, z, col0, s0, w_gate, b_gate, g_norm, t_valid, name):
    bb = rows.bb
    nseg = bb if bb > 1 else 1
    seg = ROW_BLOCK // nseg
    has_state = s0 is not None
    wg = jnp.pad(w_gate, ((0, LANES - GLA_RANK), (0, 0))).astype(BF16)
    state_spec = rows.seq(GLA_HEADS, GLA_HK, GLA_HV)
    in_specs = [rows.act(GLA_DK, col0 // GLA_DK), rows.act(GLA_DK, col0 // GLA_DK + 1),
                rows.act(GLA_DV, (col0 + 2 * GLA_DK) // GLA_DV),
                rows.act(GLA_DV, (col0 + 2 * GLA_DK) // GLA_DV + 1),
                rows.act(LANES, (col0 + 2 * GLA_DK + 2 * GLA_DV) // LANES),
                _const((LANES, GLA_DK)), _const((1, GLA_DK)), _const((1, GLA_HV))]
    args = [z, z, z, z, z, wg, b_gate.reshape(1, -1), g_norm.reshape(1, -1)]
    if has_state:
        in_specs.append(state_spec)
        args.append(s0)
    return pl.pallas_call(
        functools.partial(_gla_kernel, nseg=nseg, seg=seg, t_valid=t_valid, has_state=has_state),
        grid=rows.grid,
        in_specs=in_specs,
        out_specs=[rows.act(GLA_DV), state_spec],
        out_shape=[jax.ShapeDtypeStruct((rows.nseq, rows.T, GLA_DV), BF16),
                   jax.ShapeDtypeStruct((rows.nseq, GLA_HEADS, GLA_HK, GLA_HV), F32)],
        scratch_shapes=[pltpu.VMEM((nseg, GLA_HEADS, GLA_HV, GLA_HK), F32)],
        compiler_params=_params("parallel", "arbitrary"),
        name=name,
    )(*args)


def _sgu_kernel(zu_ref, zv_ref, w_ref, bias_ref, g_ref, b_ref, y_ref, *maybe_v_ref):
    bb, tT, _ = zu_ref.shape
    R = ROW_BLOCK
    for c in range(bb * tT // R):
        if bb == 1:
            zu = zu_ref[0, c * R:(c + 1) * R, :]
            zv = zv_ref[0, c * R:(c + 1) * R, :]
        else:
            zu = zu_ref[...].reshape(R, D_SGU)
            zv = zv_ref[...].reshape(R, D_SGU)
        v = _layer_norm(_gelu_tanh(zv), g_ref[...], b_ref[...])
        vb = v.astype(BF16)
        mixed = jnp.concatenate(
            [_mm(w_ref[g], vb[:, g * SGU_GD:(g + 1) * SGU_GD]) for g in range(SGU_GROUPS)], axis=1)
        out = (_gelu_tanh(zu) * (mixed + bias_ref[...])).astype(y_ref.dtype)
        if bb == 1:
            y_ref[0, c * R:(c + 1) * R, :] = out
        else:
            y_ref[...] = out.reshape(bb, tT, D_SGU)
        for v_ref in maybe_v_ref:
            if bb == 1:
                v_ref[0, c * R:(c + 1) * R, :] = v
            else:
                v_ref[...] = v.reshape(bb, tT, D_SGU)


def _sgu(rows, z, w_blocks, bias_rows, ln_g, ln_b, want_v, name):
    out_specs = [rows.act(D_SGU)]
    out_shape = [jax.ShapeDtypeStruct((rows.nseq, rows.T, D_SGU), BF16)]
    if want_v:
        out_specs.append(rows.act(D_SGU))
        out_shape.append(jax.ShapeDtypeStruct((rows.nseq, rows.T, D_SGU), F32))
    return pl.pallas_call(
        _sgu_kernel,
        grid=rows.grid,
        in_specs=[rows.act(D_SGU, 0), rows.act(D_SGU, 1),
                  _const((SGU_GROUPS, ROW_BLOCK, ROW_BLOCK)), _const((ROW_BLOCK, D_SGU)),
                  _const((1, D_SGU)), _const((1, D_SGU))],
        out_specs=out_specs,
        out_shape=out_shape,
        compiler_params=_params("parallel", "parallel"),
        name=name,
    )(z, z, w_blocks, bias_rows, ln_g.reshape(1, -1), ln_b.reshape(1, -1))


def _inproj_sgu_kernel(x_ref, sh_ref, sc_ref, g_ref, w_ref, ws_ref, bias_ref, lg_ref, lb_ref,
                       z_ref, y_ref, zu_scr, zv_scr, *, col_chunk):
    _, tT, d = x_ref.shape
    n_rest = z_ref.shape[-1]
    R = ROW_BLOCK
    h = _modulated(x_ref[...], g_ref[...], sc_ref[...], sh_ref[...]).reshape(tT, d).astype(BF16)
    for c0 in range(0, D_SGU, col_chunk):
        zv_scr[:, c0:c0 + col_chunk] = _mm(h, w_ref[:, D_SGU + c0:D_SGU + c0 + col_chunk])
    for c0 in range(0, D_SGU, col_chunk):
        zu_scr[:, c0:c0 + col_chunk] = _mm(h, w_ref[:, c0:c0 + col_chunk])

    def project_rest(c0):
        c1 = min(n_rest, c0 + col_chunk)
        z_ref[0, :, c0:c1] = _mm(h, w_ref[:, 2 * D_SGU + c0:2 * D_SGU + c1])

    def gate_rows(r0):
        v = _layer_norm(_gelu_tanh(zv_scr[r0:r0 + R, :]), lg_ref[...], lb_ref[...]).astype(BF16)
        mixed = jnp.concatenate(
            [_mm(ws_ref[g], v[:, g * SGU_GD:(g + 1) * SGU_GD]) for g in range(SGU_GROUPS)], axis=1)
        y_ref[0, r0:r0 + R, :] = (_gelu_tanh(zu_scr[r0:r0 + R, :]) * (mixed + bias_ref[...])).astype(y_ref.dtype)

    rest = list(range(0, n_rest, col_chunk))
    gate = list(range(0, tT, R))
    while rest or gate:
        if rest:
            project_rest(rest.pop(0))
        if gate:
            gate_rows(gate.pop(0))


def _inproj_sgu(rows, x, mods, layer, g, w, w_blocks, bias_rows, ln_g, ln_b, name):
    assert rows.bb == 1 and rows.tT % ROW_BLOCK == 0
    tT = rows.tT
    n_rest = w.shape[1] - 2 * D_SGU
    return pl.pallas_call(
        functools.partial(_inproj_sgu_kernel, col_chunk=512),
        grid=rows.grid,
        in_specs=[rows.act(D_MODEL), rows.mod(layer, 0), rows.mod(layer, 1),
                  _layer_block(layer, (1, 1, D_MODEL)),
                  pl.BlockSpec(w.shape, lambda *_: (0, 0), pipeline_mode=pl.Buffered(1)),
                  _const((SGU_GROUPS, ROW_BLOCK, ROW_BLOCK)), _const((ROW_BLOCK, D_SGU)),
                  _const((1, D_SGU)), _const((1, D_SGU))],
        out_specs=[rows.act(n_rest), rows.act(D_SGU)],
        out_shape=[jax.ShapeDtypeStruct((rows.nseq, rows.T, n_rest), F32),
                   jax.ShapeDtypeStruct((rows.nseq, rows.T, D_SGU), BF16)],
        scratch_shapes=[pltpu.VMEM((tT, D_SGU), F32), pltpu.VMEM((tT, D_SGU), F32)],
        compiler_params=_params("parallel", "parallel"),
        name=name,
    )(x, mods, mods, g.reshape(-1, 1, 1, D_MODEL), w, w_blocks, bias_rows,
      ln_g.reshape(1, -1), ln_b.reshape(1, -1))


def _swa_kernel(sink_ref, q_ref, kc_ref, vc_ref, kp_ref, vp_ref, y_ref, *, prev_from_grid):
    bb, tile_rows, _ = q_ref.shape
    W = WINDOW
    rq = min(tile_rows, W)
    nblk = tile_rows // rq
    assert nblk == 1 or bb == 1
    tq = lax.broadcasted_iota(jnp.int32, (rq, 2 * W), 0)
    sk = lax.broadcasted_iota(jnp.int32, (rq, 2 * W), 1)
    dist_i = tq + W - sk
    in_window = (dist_i >= 0) & (dist_i < W)
    first_key = jnp.where(pl.program_id(1) > 0, 0, W) if prev_from_grid else 0
    dist = dist_i.astype(F32)
    scale = ATT_HD ** -0.5
    for b, j in [(b, j) for b in range(bb) for j in range(nblk)]:
        rows = slice(j * rq, (j + 1) * rq)
        q = q_ref[b, rows, :]
        kc, vc = kc_ref[b, rows, :], vc_ref[b, rows, :]
        if rq < W:
            pad = jnp.zeros((W - rq, KV_WIDTH), F32)
            kc = jnp.concatenate([kc, pad], axis=0)
            vc = jnp.concatenate([vc, pad], axis=0)
        if j == 0:
            kp, vp = kp_ref[b], vp_ref[b]
            valid = in_window & (sk >= first_key)
        else:
            prev_rows = slice((j - 1) * rq, j * rq)
            kp, vp = kc_ref[b, prev_rows, :], vc_ref[b, prev_rows, :]
            valid = in_window
        for g in range(ATT_KV_HEADS):
            gs = slice(g * ATT_HD, (g + 1) * ATT_HD)
            qg = jnp.concatenate(
                [q[:, (g * ATT_REP + r) * ATT_HD:(g * ATT_REP + r + 1) * ATT_HD] for r in range(ATT_REP)],
                axis=0).astype(BF16)
            k_all = jnp.concatenate([kp[:, gs], kc[:, gs]], axis=0).astype(BF16)
            v_all = jnp.concatenate([vp[:, gs], vc[:, gs]], axis=0).astype(BF16)
            s_all = _mm_nt(qg, k_all) * scale
            probs, inv = [], []
            for r in range(ATT_REP):
                h = g * ATT_REP + r
                sink = sink_ref[h]
                s = jnp.where(valid, s_all[r * rq:(r + 1) * rq, :] - ALIBI_SLOPES[h] * dist, NEG_INF)
                m = jnp.maximum(jnp.max(s, axis=-1, keepdims=True), sink)
                p = jnp.exp(s - m)
                inv.append(1.0 / (jnp.sum(p, axis=-1, keepdims=True) + jnp.exp(sink - m)))
                probs.append(p.astype(BF16))
            p_all = jnp.concatenate(probs, axis=0)
            o = _mm(p_all, v_all)
            for r in range(ATT_REP):
                h = g * ATT_REP + r
                y_ref[b, rows, h * ATT_HD:(h + 1) * ATT_HD] = (o[r * rq:(r + 1) * rq, :] * inv[r]).astype(y_ref.dtype)


def _swa(rows, z, col0, sinks, k_prev, v_prev, name):
    bb, tT = rows.bb, rows.tT
    kcol = (col0 + ATT_WIDTH) // KV_WIDTH
    from_grid = k_prev is None
    if from_grid:
        assert tT % WINDOW == 0 and bb == 1
        nw = tT // WINDOW
        kp_spec = pl.BlockSpec((1, WINDOW, KV_WIDTH), lambda i, t: (i, jnp.maximum(t * nw - 1, 0), kcol))
        vp_spec = pl.BlockSpec((1, WINDOW, KV_WIDTH), lambda i, t: (i, jnp.maximum(t * nw - 1, 0), kcol + 1))
        k_prev, v_prev = z, z
    else:
        kp_spec = vp_spec = rows.seq(WINDOW, KV_WIDTH)
    return pl.pallas_call(
        functools.partial(_swa_kernel, prev_from_grid=from_grid),
        grid=rows.grid,
        in_specs=[pl.BlockSpec(memory_space=pltpu.SMEM),
                  rows.act(ATT_WIDTH, col0 // ATT_WIDTH), rows.act(KV_WIDTH, kcol),
                  rows.act(KV_WIDTH, kcol + 1), kp_spec, vp_spec],
        out_specs=rows.act(ATT_WIDTH),
        out_shape=jax.ShapeDtypeStruct((rows.nseq, rows.T, ATT_WIDTH), BF16),
        compiler_params=_params("parallel", "arbitrary"),
        name=name,
    )(sinks, z, z, z, k_prev, v_prev)


def kernel(x_prompt, x_sample, c_prompt, c_sample, state_conv, state_gla, cache_k, cache_v, w_ada, b_ada, g_mix, g_ffn, w_in_ab, conv_w, conv_b, conv_ln_g, conv_ln_b, gla_w_gate, gla_b_gate, gla_g_norm, w_in_cd, sgu_w, sgu_b, sgu_ln_g, sgu_ln_b, att_sinks, w_out, w_ffn_gate, w_ffn_up, w_ffn_down, g_final):
    nb, seq, d = x_prompt.shape
    ndb, dseq, _ = x_sample.shape
    ns = ndb * dseq
    assert d == D_MODEL and dseq <= SAMPLE_ROWS and seq % 512 == 0 and ns % 256 == 0
    sb = ROW_BLOCK // SAMPLE_ROWS
    assert ndb % sb == 0

    w_ab = jnp.pad(w_in_ab, ((0, 0), (0, IN_AB_PAD - IN_AB))).astype(BF16)
    w_cd = w_in_cd.astype(BF16)
    w_o = w_out.astype(BF16)
    w_fg, w_fu, w_fd = w_ffn_gate.astype(BF16), w_ffn_up.astype(BF16), w_ffn_down.astype(BF16)

    tri = jnp.tril(jnp.ones((ROW_BLOCK, ROW_BLOCK), bool))
    sgu_wp = jnp.where(tri, sgu_w, 0.0).astype(BF16)
    sgu_ws = jnp.where(tri, sgu_w, 0.0)[:, :SAMPLE_ROWS, :SAMPLE_ROWS]
    sgu_ws = jnp.einsum("ab,gts->gatbs", jnp.eye(sb, dtype=F32), sgu_ws).reshape(
        SGU_GROUPS, ROW_BLOCK, ROW_BLOCK).astype(BF16)
    sgu_bias_p = jnp.repeat(sgu_b.T, SGU_GD, axis=1)
    sgu_bias_s = jnp.tile(sgu_bias_p[:SAMPLE_ROWS], (sb, 1))

    c_all = jnp.concatenate([jnp.repeat(c_sample, dseq, axis=0), c_prompt], axis=0)
    mods_rows, mods_p = _ada(c_all, ns, w_ada, b_ada)
    mods_s = mods_rows.reshape(mods_rows.shape[0], 1, ns, N_MODS * D_MODEL)

    rp_dense = _Rows(nb, seq, 1, 512)
    rp_ffn = _Rows(nb, seq, 1, min(seq, FFN_ROWS))
    rs_dense = _Rows(1, ns, 1, min(ns, 512), per_row=True)
    rp_in = _Rows(nb, seq, 1, 256)
    rs_in = _Rows(1, ns, 1, 256, per_row=True)
    rp_mix = _Rows(nb, seq, 1, 256)
    rp_att = _Rows(nb, seq, 1, WINDOW)
    rs_mix = _Rows(ndb, SAMPLE_ROWS, sb, SAMPLE_ROWS)

    def to_mixer(z):
        z = z.reshape(ndb, dseq, z.shape[-1])
        return jnp.pad(z, ((0, 0), (0, SAMPLE_ROWS - dseq), (0, 0)))

    def to_dense(o):
        return o[:, :dseq].reshape(1, ns, o.shape[-1])

    yp, ys = x_prompt, x_sample.reshape(1, ns, d)

    conv0 = jnp.zeros((nb, CONV_HIST, D_CONV), F32)
    zp, oa_p, st_conv_p = _inproj_conv(rp_in, yp, mods_p, 0, g_mix, w_ab, conv0, conv_w, conv_b,
                                       conv_ln_g, conv_ln_b, "inproj_conv_p")
    zs = to_mixer(_inproj(rs_in, ys, mods_s, 0, g_mix, w_ab, "inproj_ab_s"))
    oa_s, st_conv_s = _conv(rs_mix, zs, state_conv, conv_w, conv_b, conv_ln_g, conv_ln_b, dseq, "conv_s")
    ob_p, st_gla_p = _gla(rp_mix, zp, 0, None, gla_w_gate, gla_b_gate, gla_g_norm, ROW_BLOCK, "gla_p")
    ob_s, st_gla_s = _gla(rs_mix, zs, 2 * D_CONV, state_gla, gla_w_gate, gla_b_gate, gla_g_norm, dseq, "gla_s")
    yp = _outproj(rp_dense, oa_p, ob_p, yp, mods_p, 0, w_o, "outproj0_p")
    ys = _outproj(rs_dense, to_dense(oa_s), to_dense(ob_s), ys, mods_s, 0, w_o, "outproj0_s")
    yp = _ffn(rp_ffn, yp, mods_p, 0, g_ffn, w_fg, w_fu, w_fd, g_final, False, "ffn0_p")
    ys = _ffn(rs_dense, ys, mods_s, 0, g_ffn, w_fg, w_fu, w_fd, g_final, False, "ffn0_s")

    zp, oc_p = _inproj_sgu(rp_in, yp, mods_p, 1, g_mix, w_cd, sgu_wp, sgu_bias_p, sgu_ln_g, sgu_ln_b,
                           "inproj_sgu_p")
    zs = to_mixer(_inproj(rs_in, ys, mods_s, 1, g_mix, w_cd, "inproj_cd_s"))
    oc_s, sgu_v = _sgu(rs_mix, zs, sgu_ws, sgu_bias_s, sgu_ln_g, sgu_ln_b, True, "sgu_s")
    wb = cache_k.shape[1]
    od_p = _swa(rp_att, zp, 0, att_sinks, None, None, "swa_p")
    od_s = _swa(rs_mix, zs, 2 * D_SGU, att_sinks, cache_k.reshape(ndb, wb, KV_WIDTH),
                cache_v.reshape(ndb, wb, KV_WIDTH), "swa_s")
    yp = _outproj(rp_dense, oc_p, od_p, yp, mods_p, 1, w_o, "outproj1_p")
    ys = _outproj(rs_dense, to_dense(oc_s), to_dense(od_s), ys, mods_s, 1, w_o, "outproj1_s")
    yp = _ffn(rp_ffn, yp, mods_p, 1, g_ffn, w_fg, w_fu, w_fd, g_final, True, "ffn1_p")
    ys = _ffn(rs_dense, ys, mods_s, 1, g_ffn, w_fg, w_fu, w_fd, g_final, True, "ffn1_s")

    k0 = 2 * D_SGU + ATT_WIDTH
    kv_shape = (ATT_KV_HEADS, ATT_HD)
    kp0 = ATT_WIDTH
    k_new_p = zp[:, seq - wb:, kp0:kp0 + KV_WIDTH].reshape(nb, wb, *kv_shape)
    v_new_p = zp[:, seq - wb:, kp0 + KV_WIDTH:kp0 + 2 * KV_WIDTH].reshape(nb, wb, *kv_shape)
    k_new_s = zs[:, :dseq, k0:k0 + KV_WIDTH].reshape(ndb, dseq, *kv_shape)
    v_new_s = zs[:, :dseq, k0 + KV_WIDTH:k0 + 2 * KV_WIDTH].reshape(ndb, dseq, *kv_shape)
    k_s = jnp.concatenate([cache_k, k_new_s], axis=1)[:, -wb:]
    v_s = jnp.concatenate([cache_v, v_new_s], axis=1)[:, -wb:]
    return (yp, ys.reshape(ndb, dseq, d), st_conv_p, st_conv_s, st_gla_p, st_gla_s, sgu_v[:, :dseq],
            k_new_p, k_s, v_new_p, v_s)
```

```python
import functools

import jax
import jax.numpy as jnp
from jax import lax
from jax.experimental import pallas as pl
from jax.experimental.pallas import tpu as pltpu

F32 = jnp.float32
BF16 = jnp.bfloat16

D_MODEL = 2048
NORM_EPS = 1e-6
NEG_INF = -1e30
D_CONV = 1024
CONV_WIDTH = 31
CONV_HIST = CONV_WIDTH - 1
GLA_HEADS = 4
GLA_DV = 1024
GLA_DK = 512
GLA_HK = 128
GLA_HV = 256
GLA_RANK = 16
GLA_GATE_NORM = 16.0
D_SGU = 1024
SGU_GROUPS = 4
SGU_GD = 256
ATT_HD = 64
ATT_HEADS = 16
ATT_KV_HEADS = 2
ATT_REP = 8
ATT_WIDTH = 1024
KV_WIDTH = 128
WINDOW = 128
D_FF = 5632
IN_AB = 5136
IN_AB_PAD = 5248
N_MODS = 6

LANES = 128
SUBLANES = 8
SAMPLE_ROWS = SUBLANES
ROW_BLOCK = 128
PROJ_ROWS = 256
DENSE_ROWS = 512
FFN_CHUNK = 512
FFN_ROWS = 1024
V7X_VMEM_BYTES = 64 * 1024 * 1024
FFN_VMEM_LIMIT = V7X_VMEM_BYTES - 1024 * 1024
VMEM_LIMIT = V7X_VMEM_BYTES - 8 * 1024 * 1024

ALIBI_SLOPES = tuple(2.0 ** (-8.0 * (h + 1) / ATT_HEADS) for h in range(ATT_HEADS))


def _mm(a, b):
    return jnp.dot(a, b, preferred_element_type=F32)


def _mm_nt(a, b):
    return lax.dot_general(a, b, (((1,), (1,)), ((), ())), preferred_element_type=F32)


def _sigmoid(x):
    return 1.0 / (1.0 + jnp.exp(-x))


def _silu(x):
    return x * _sigmoid(x)


def _gelu_tanh(x):
    c = 0.7978845608028654
    return 0.5 * x * (1.0 + jnp.tanh(c * (x + 0.044715 * (x * x * x))))


def _log_sigmoid(x):
    return jnp.minimum(x, 0.0) - jnp.log(1.0 + jnp.exp(-jnp.abs(x)))


def _rms(x):
    return x * lax.rsqrt(jnp.mean(x * x, axis=-1, keepdims=True) + NORM_EPS)


def _layer_norm(x, g, b):
    mu = jnp.mean(x, axis=-1, keepdims=True)
    xc = x - mu
    return xc * lax.rsqrt(jnp.mean(xc * xc, axis=-1, keepdims=True) + NORM_EPS) * g + b


def _params(*sem):
    return pltpu.CompilerParams(dimension_semantics=sem, vmem_limit_bytes=VMEM_LIMIT)


def _const(shape):
    nd = len(shape)
    return pl.BlockSpec(shape, lambda *_: (0,) * nd)


def _layer_block(layer, tail, resident=False):
    nz = (0,) * len(tail)
    mode = dict(pipeline_mode=pl.Buffered(1)) if resident else {}
    return pl.BlockSpec((None,) + tuple(tail), lambda *_: (layer,) + nz, **mode)


def _ada_kernel(c_ref, w_ref, b_ref, rows_ref, seq_ref, a_scr):
    @pl.when((pl.program_id(0) == 0) & (pl.program_id(1) == 0))
    def _():
        a_scr[...] = _silu(c_ref[...]).astype(BF16)

    res = _mm(a_scr[...], w_ref[0].astype(BF16)) + b_ref[0]
    ns = rows_ref.shape[1]
    rows_ref[0] = res[:ns]
    for r in range(seq_ref.shape[1]):
        seq_ref[0, r] = res[ns + r:ns + r + 1]


def _ada(c_all, n_rows, w_ada, b_ada, tn=1024):
    depth, d, n = w_ada.shape
    rows = c_all.shape[0]
    nb = rows - n_rows
    return pl.pallas_call(
        _ada_kernel,
        grid=(depth, n // tn),
        in_specs=[pl.BlockSpec((rows, d), lambda l, j: (0, 0)),
                  pl.BlockSpec((1, d, tn), lambda l, j: (l, 0, j)),
                  pl.BlockSpec((1, 1, tn), lambda l, j: (l, 0, j))],
        out_specs=[pl.BlockSpec((1, n_rows, tn), lambda l, j: (l, 0, j)),
                   pl.BlockSpec((1, nb, 1, tn), lambda l, j: (l, 0, 0, j))],
        out_shape=[jax.ShapeDtypeStruct((depth, n_rows, n), F32),
                   jax.ShapeDtypeStruct((depth, nb, 1, n), F32)],
        scratch_shapes=[pltpu.VMEM((rows, d), BF16)],
        compiler_params=_params("arbitrary", "arbitrary"),
        name="ada",
    )(c_all, w_ada, b_ada.reshape(depth, 1, n))


class _Rows:
    def __init__(self, nseq, T, bb, tT, per_row=False):
        assert nseq % bb == 0 and T % tT == 0 and (nseq == 1 or not per_row)
        self.nseq, self.T, self.bb, self.tT = nseq, T, bb, tT
        self.nt = T // tT
        self.grid = (nseq // bb, self.nt)
        self.ntiles = self.grid[0] * self.nt
        self.per_row = per_row

    def _tile(self, tile):
        nt = self.nt
        return tile if tile is not None else (lambda i, t, *_: i * nt + t)

    def act(self, width, col=0, tile=None):
        nt, fn = self.nt, self._tile(tile)
        return pl.BlockSpec((self.bb, self.tT, width), lambda *g: (fn(*g) // nt, fn(*g) % nt, col))

    def mod(self, layer, k, tile=None):
        nt, fn = self.nt, self._tile(tile)
        if self.per_row:
            return pl.BlockSpec((None, 1, self.tT, D_MODEL), lambda *g: (layer, 0, fn(*g) % nt, k))
        return pl.BlockSpec((None, self.bb, 1, D_MODEL), lambda *g: (layer, fn(*g) // nt, 0, k))

    def seq(self, *tail):
        nz = (0,) * len(tail)
        return pl.BlockSpec((self.bb,) + tuple(tail), lambda i, t, *_: (i,) + nz)


def _modulated(x, g, scale, shift):
    return (_rms(x) * g) * (1.0 + scale) + shift


def _inproj_kernel(x_ref, sh_ref, sc_ref, g_ref, w_ref, z_ref, *, col_chunk):
    bb, tT, d = x_ref.shape
    h = _modulated(x_ref[...], g_ref[...], sc_ref[...], sh_ref[...])
    h = h.reshape(bb * tT, d).astype(BF16)
    n = w_ref.shape[1]
    for c0 in range(0, n, col_chunk):
        c1 = min(n, c0 + col_chunk)
        z_ref[:, :, c0:c1] = _mm(h, w_ref[:, c0:c1]).reshape(bb, tT, c1 - c0)


def _inproj(rows, x, mods, layer, g, w, name):
    n = w.shape[1]
    return pl.pallas_call(
        functools.partial(_inproj_kernel, col_chunk=512),
        grid=rows.grid,
        in_specs=[rows.act(D_MODEL), rows.mod(layer, 0), rows.mod(layer, 1),
                  _layer_block(layer, (1, 1, D_MODEL)),
                  pl.BlockSpec((D_MODEL, n), lambda *_: (0, 0), pipeline_mode=pl.Buffered(1))],
        out_specs=rows.act(n),
        out_shape=jax.ShapeDtypeStruct((rows.nseq, rows.T, n), F32),
        compiler_params=_params("parallel", "parallel"),
        name=name,
    )(x, mods, mods, g.reshape(-1, 1, 1, D_MODEL), w)


def _outproj_kernel(oa_ref, ob_ref, y_ref, gate_ref, w_ref, o_ref):
    bb, tT, d = y_ref.shape
    half = oa_ref.shape[-1]
    oa = oa_ref[...].reshape(bb * tT, half)
    ob = ob_ref[...].reshape(bb * tT, half)
    acc = _mm(oa, w_ref[:half, :]) + _mm(ob, w_ref[half:, :])
    o_ref[...] = y_ref[...] + gate_ref[...] * acc.reshape(bb, tT, d)


def _outproj(rows, oa, ob, y, mods, layer, w, name):
    half = oa.shape[-1]
    return pl.pallas_call(
        _outproj_kernel,
        grid=rows.grid,
        in_specs=[rows.act(half), rows.act(half), rows.act(D_MODEL), rows.mod(layer, 2),
                  _layer_block(layer, (D_MODEL, D_MODEL), resident=True)],
        out_specs=rows.act(D_MODEL),
        out_shape=jax.ShapeDtypeStruct(y.shape, F32),
        compiler_params=_params("parallel", "parallel"),
        name=name,
    )(oa, ob, y, mods, w)


def _ffn_kernel(y_ref, sh_ref, sc_ref, gt_ref, g_ref, wg_ref, wu_ref, wd_ref, gf_ref, o_ref,
                h_scr, *, final):
    f = pl.program_id(2)
    bb, tT, d = y_ref.shape

    @pl.when(f == 0)
    def _():
        h = _modulated(y_ref[...], g_ref[...], sc_ref[...], sh_ref[...])
        h_scr[...] = h.reshape(bb * tT, d).astype(BF16)
        o_ref[...] = jnp.zeros_like(o_ref)

    h = h_scr[...]
    a = _mm(h, wg_ref[...])
    b = _mm(h, wu_ref[...])
    p = (_silu(a) * b).astype(BF16)
    o_ref[...] += _mm(p, wd_ref[...]).reshape(bb, tT, d)

    @pl.when(f == pl.num_programs(2) - 1)
    def _():
        out = y_ref[...] + gt_ref[...] * o_ref[...]
        if final:
            out = _rms(out) * gf_ref[...]
        o_ref[...] = out


def _ffn(rows, y, mods, layer, g, wg, wu, wd, g_final, final, name, tf=FFN_CHUNK):
    nf = D_FF // tf
    m = rows.bb * rows.tT
    return pl.pallas_call(
        functools.partial(_ffn_kernel, final=final),
        grid=rows.grid + (nf,),
        in_specs=[rows.act(D_MODEL), rows.mod(layer, 3), rows.mod(layer, 4), rows.mod(layer, 5),
                  _layer_block(layer, (1, 1, D_MODEL)),
                  pl.BlockSpec((None, D_MODEL, tf), lambda i, t, f: (layer, 0, f)),
                  pl.BlockSpec((None, D_MODEL, tf), lambda i, t, f: (layer, 0, f)),
                  pl.BlockSpec((None, tf, D_MODEL), lambda i, t, f: (layer, f, 0)),
                  _const((1, 1, D_MODEL))],
        out_specs=rows.act(D_MODEL),
        out_shape=jax.ShapeDtypeStruct(y.shape, F32),
        scratch_shapes=[pltpu.VMEM((m, D_MODEL), BF16)],
        compiler_params=pltpu.CompilerParams(dimension_semantics=("parallel", "parallel", "arbitrary"),
                                             vmem_limit_bytes=FFN_VMEM_LIMIT),
        name=name,
    )(y, mods, mods, mods, g.reshape(-1, 1, 1, D_MODEL), wg, wu, wd, g_final.reshape(1, 1, D_MODEL))


CONV_PAD = 32


def _conv_kernel(za_ref, prev_ref, w_ref, cb_ref, g_ref, b_ref, y_ref, st_ref, ubuf, ybuf,
                 *, t_valid, rc, lc):
    t = pl.program_id(1)
    bb, tT, _ = za_ref.shape
    lo = CONV_PAD - CONV_HIST
    assert lc == LANES
    nlc = D_CONV // lc

    @pl.when(t == 0)
    def _():
        for b in range(bb):
            for ci in range(nlc):
                ubuf[b * nlc + ci, lo:CONV_PAD, :] = prev_ref[b, :, ci * lc:(ci + 1) * lc]

    @pl.when(t > 0)
    def _():
        ubuf[:, lo:CONV_PAD, :] = ubuf[:, tT + lo:tT + CONV_PAD, :]

    for b in range(bb):
        for ci in range(nlc):
            c0 = ci * lc
            ubuf[b * nlc + ci, CONV_PAD:CONV_PAD + tT, :] = (
                za_ref[b, :, c0:c0 + lc] * _sigmoid(za_ref[b, :, D_CONV + c0:D_CONV + c0 + lc]))

    for b in range(bb):
        def row_chunk(i, carry, b=b):
            r0 = pl.multiple_of(i * rc, SUBLANES)
            for ci in range(nlc):
                c0 = ci * lc
                acc = jnp.broadcast_to(cb_ref[:, c0:c0 + lc], (rc, lc))
                for j in range(CONV_WIDTH):
                    acc = acc + w_ref[j:j + 1, c0:c0 + lc] * ubuf[b * nlc + ci, pl.ds(r0 + lo + j, rc), :]
                ybuf[b, pl.ds(r0, rc), c0:c0 + lc] = acc
            return carry
        lax.fori_loop(0, tT // rc, row_chunk, 0)

    y = _layer_norm(ybuf[...], g_ref[...], b_ref[...])
    y_ref[...] = _silu(y).astype(y_ref.dtype)

    @pl.when(t == pl.num_programs(1) - 1)
    def _():
        for b in range(bb):
            for ci in range(nlc):
                st_ref[b, :, ci * lc:(ci + 1) * lc] = ubuf[b * nlc + ci, t_valid + lo:t_valid + CONV_PAD, :]


def _conv(rows, z, prev, conv_w, conv_b, ln_g, ln_b, t_valid, name):
    bb, tT = rows.bb, rows.tT
    rc = min(tT, 64)
    return pl.pallas_call(
        functools.partial(_conv_kernel, t_valid=t_valid, rc=rc, lc=LANES),
        grid=rows.grid,
        in_specs=[rows.act(2 * D_CONV, 0), rows.seq(CONV_HIST, D_CONV),
                  _const((CONV_WIDTH, D_CONV)), _const((1, D_CONV)), _const((1, D_CONV)), _const((1, D_CONV))],
        out_specs=[rows.act(D_CONV), rows.seq(CONV_HIST, D_CONV)],
        out_shape=[jax.ShapeDtypeStruct((rows.nseq, rows.T, D_CONV), BF16),
                   jax.ShapeDtypeStruct((rows.nseq, CONV_HIST, D_CONV), F32)],
        scratch_shapes=[pltpu.VMEM((bb * (D_CONV // LANES), CONV_PAD + tT, LANES), F32),
                        pltpu.VMEM((bb, tT, D_CONV), F32)],
        compiler_params=_params("parallel", "arbitrary"),
        name=name,
    )(z, prev, conv_w, conv_b.reshape(1, -1), ln_g.reshape(1, -1), ln_b.reshape(1, -1))


def _inproj_conv_kernel(x_ref, sh_ref, sc_ref, g_ref, w_ref, prev_ref, cw_ref, cb_ref, lg_ref, lb_ref,
                        z_ref, y_ref, st_ref, ubuf, ybuf, *, rc, col_chunk):
    t = pl.program_id(1)
    _, tT, d = x_ref.shape
    lo = CONV_PAD - CONV_HIST
    lc = LANES
    nlc = D_CONV // lc
    n_rest = z_ref.shape[-1]

    @pl.when(t == 0)
    def _():
        for ci in range(nlc):
            ubuf[ci, lo:CONV_PAD, :] = prev_ref[0, :, ci * lc:(ci + 1) * lc]

    @pl.when(t > 0)
    def _():
        ubuf[:, lo:CONV_PAD, :] = ubuf[:, tT + lo:tT + CONV_PAD, :]

    h = _modulated(x_ref[...], g_ref[...], sc_ref[...], sh_ref[...]).reshape(tT, d).astype(BF16)

    for c0 in range(0, D_CONV, col_chunk):
        u = _mm(h, w_ref[:, c0:c0 + col_chunk]) * _sigmoid(_mm(h, w_ref[:, D_CONV + c0:D_CONV + c0 + col_chunk]))
        for k in range(col_chunk // lc):
            ubuf[c0 // lc + k, CONV_PAD:CONV_PAD + tT, :] = u[:, k * lc:(k + 1) * lc]

    def project_rest(c0):
        c1 = min(n_rest, c0 + col_chunk)
        z_ref[0, :, c0:c1] = _mm(h, w_ref[:, 2 * D_CONV + c0:2 * D_CONV + c1])

    def conv_rows(r0):
        for ci in range(nlc):
            c0 = ci * lc
            acc = jnp.broadcast_to(cb_ref[:, c0:c0 + lc], (rc, lc))
            for j in range(CONV_WIDTH):
                acc = acc + cw_ref[j:j + 1, c0:c0 + lc] * ubuf[ci, r0 + lo + j:r0 + lo + j + rc, :]
            ybuf[r0:r0 + rc, c0:c0 + lc] = acc
        y = _layer_norm(ybuf[r0:r0 + rc, :], lg_ref[...], lb_ref[...])
        y_ref[0, r0:r0 + rc, :] = _silu(y).astype(y_ref.dtype)

    rest = list(range(0, n_rest, col_chunk))
    conv = list(range(0, tT, rc))
    while rest or conv:
        if rest:
            project_rest(rest.pop(0))
        if conv:
            conv_rows(conv.pop(0))

    @pl.when(t == pl.num_programs(1) - 1)
    def _():
        for ci in range(nlc):
            st_ref[0, :, ci * lc:(ci + 1) * lc] = ubuf[ci, tT + lo:tT + CONV_PAD, :]


def _inproj_conv(rows, x, mods, layer, g, w, prev, conv_w, conv_b, ln_g, ln_b, name):
    assert rows.bb == 1
    tT = rows.tT
    n_rest = w.shape[1] - 2 * D_CONV
    return pl.pallas_call(
        functools.partial(_inproj_conv_kernel, rc=32, col_chunk=512),
        grid=rows.grid,
        in_specs=[rows.act(D_MODEL), rows.mod(layer, 0), rows.mod(layer, 1),
                  _layer_block(layer, (1, 1, D_MODEL)),
                  pl.BlockSpec(w.shape, lambda *_: (0, 0), pipeline_mode=pl.Buffered(1)),
                  rows.seq(CONV_HIST, D_CONV),
                  _const((CONV_WIDTH, D_CONV)), _const((1, D_CONV)), _const((1, D_CONV)), _const((1, D_CONV))],
        out_specs=[rows.act(n_rest), rows.act(D_CONV), rows.seq(CONV_HIST, D_CONV)],
        out_shape=[jax.ShapeDtypeStruct((rows.nseq, rows.T, n_rest), F32),
                   jax.ShapeDtypeStruct((rows.nseq, rows.T, D_CONV), BF16),
                   jax.ShapeDtypeStruct((rows.nseq, CONV_HIST, D_CONV), F32)],
        scratch_shapes=[pltpu.VMEM((D_CONV // LANES, CONV_PAD + tT, LANES), F32),
                        pltpu.VMEM((tT, D_CONV), F32)],
        compiler_params=_params("parallel", "arbitrary"),
        name=name,
    )(x, mods, mods, g.reshape(-1, 1, 1, D_MODEL), w, prev, conv_w, conv_b.reshape(1, -1),
      ln_g.reshape(1, -1), ln_b.reshape(1, -1))


def _segment_cumsum(x, seg):
    pos = lax.broadcasted_iota(jnp.int32, x.shape, 0) % seg
    s = 1
    while s < seg:
        x = x + jnp.where(pos >= s, pltpu.roll(x, s, 0), 0.0)
        s *= 2
    return x


def _gla_kernel(*refs, nseg, seg, t_valid, has_state):
    if has_state:
        (zq_ref, zk_ref, zv_ref, zg_ref, zr_ref, wg_ref, bg_ref, gn_ref, s0_ref,
         y_ref, st_ref, st_scr) = refs
    else:
        (zq_ref, zk_ref, zv_ref, zg_ref, zr_ref, wg_ref, bg_ref, gn_ref,
         y_ref, st_ref, st_scr) = refs
    t = pl.program_id(1)
    bb, tT, _ = zq_ref.shape
    R = ROW_BLOCK
    assert nseg * seg == R and (bb * tT) % R == 0

    @pl.when(t == 0)
    def _():
        for g in range(nseg):
            for h in range(GLA_HEADS):
                st_scr[g, h] = s0_ref[g, h].T if has_state else jnp.zeros((GLA_HV, GLA_HK), F32)

    def rows_of(ref, c):
        if nseg == 1:
            return ref[0, c * R:(c + 1) * R, :]
        return ref[...].reshape(R, ref.shape[-1])

    row = lax.broadcasted_iota(jnp.int32, (R, R), 0)
    col = lax.broadcasted_iota(jnp.int32, (R, R), 1)
    causal = (row >= col) & ((row // seg) == (col // seg))
    pos = lax.broadcasted_iota(jnp.int32, (R, GLA_DK), 0) % seg
    colseg = lax.broadcasted_iota(jnp.int32, (GLA_HV, R), 1) // seg

    for c in range(bb * tT // R):
        zr = rows_of(zr_ref, c).astype(BF16)
        la = _log_sigmoid(_mm(zr, wg_ref[...]) + bg_ref[...]) * (1.0 / GLA_GATE_NORM)
        q = rows_of(zq_ref, c) * (GLA_HK ** -0.5)
        k = rows_of(zk_ref, c)
        v = rows_of(zv_ref, c)
        zg = rows_of(zg_ref, c)
        if t_valid < seg:
            la = jnp.where(pos < t_valid, la, 0.0)
            k = jnp.where(pos < t_valid, k, 0.0)
        bcum = _segment_cumsum(la, seg)
        b3 = bcum.reshape(nseg, seg, GLA_DK)
        bl3 = b3[:, seg - 1:seg, :]
        k_last = k * jnp.exp(bl3 - b3).reshape(R, GLA_DK)
        dec3 = jnp.exp(bl3)
        q_dec = q * jnp.exp(bcum)
        bmid = b3[:, seg // 2 - 1:seg // 2, :] if nseg == 1 else jnp.zeros_like(bl3)
        q_in = (q * jnp.exp(b3 - bmid).reshape(R, GLA_DK)).astype(BF16)
        k_in = (k * jnp.exp(bmid - b3).reshape(R, GLA_DK)).astype(BF16)
        q_dec = q_dec.astype(BF16)
        k_last = k_last.astype(BF16)
        for h in range(GLA_HEADS):
            ks = slice(h * GLA_HK, (h + 1) * GLA_HK)
            vs = slice(h * GLA_HV, (h + 1) * GLA_HV)
            v_h = v[:, vs]
            att = jnp.where(causal, _mm_nt(q_in[:, ks], k_in[:, ks]), 0.0)
            o = _mm(att.astype(BF16), v_h.astype(BF16))
            vt = v_h.T
            inter = []
            for g in range(nseg):
                st = st_scr[g, h]
                inter.append(_mm_nt(q_dec[g * seg:(g + 1) * seg, ks], st.astype(BF16)))
                vt_g = vt if nseg == 1 else jnp.where(colseg == g, vt, 0.0)
                st_scr[g, h] = st * dec3[g, :, ks] + _mm(vt_g.astype(BF16), k_last[:, ks])
            o = o + (inter[0] if nseg == 1 else jnp.concatenate(inter, axis=0))
            o = _rms(o) * gn_ref[...] * _silu(zg[:, vs])
            o = o.astype(y_ref.dtype)
            if nseg == 1:
                y_ref[0, c * R:(c + 1) * R, vs] = o
            else:
                y_ref[:, :, vs] = o.reshape(bb, tT, GLA_HV)

    @pl.when(t == pl.num_programs(1) - 1)
    def _():
        for g in range(nseg):
            for h in range(GLA_HEADS):
                st_ref[g, h] = st_scr[g, h].T


def _gla(rows, z, col0, s0, w_gate, b_gate, g_norm, t_valid, name):
    bb = rows.bb
    nseg = bb if bb > 1 else 1
    seg = ROW_BLOCK // nseg
    has_state = s0 is not None
    wg = jnp.pad(w_gate, ((0, LANES - GLA_RANK), (0, 0))).astype(BF16)
    state_spec = rows.seq(GLA_HEADS, GLA_HK, GLA_HV)
    in_specs = [rows.act(GLA_DK, col0 // GLA_DK), rows.act(GLA_DK, col0 // GLA_DK + 1),
                rows.act(GLA_DV, (col0 + 2 * GLA_DK) // GLA_DV),
                rows.act(GLA_DV, (col0 + 2 * GLA_DK) // GLA_DV + 1),
                rows.act(LANES, (col0 + 2 * GLA_DK + 2 * GLA_DV) // LANES),
                _const((LANES, GLA_DK)), _const((1, GLA_DK)), _const((1, GLA_HV))]
    args = [z, z, z, z, z, wg, b_gate.reshape(1, -1), g_norm.reshape(1, -1)]
    if has_state:
        in_specs.append(state_spec)
        args.append(s0)
    return pl.pallas_call(
        functools.partial(_gla_kernel, nseg=nseg, seg=seg, t_valid=t_valid, has_state=has_state),
        grid=rows.grid,
        in_specs=in_specs,
        out_specs=[rows.act(GLA_DV), state_spec],
        out_shape=[jax.ShapeDtypeStruct((rows.nseq, rows.T, GLA_DV), BF16),
                   jax.ShapeDtypeStruct((rows.nseq, GLA_HEADS, GLA_HK, GLA_HV), F32)],
        scratch_shapes=[pltpu.VMEM((nseg, GLA_HEADS, GLA_HV, GLA_HK), F32)],
        compiler_params=_params("parallel", "arbitrary"),
        name=name,
    )(*args)


def _sgu_kernel(zu_ref, zv_ref, w_ref, bias_ref, g_ref, b_ref, y_ref, *maybe_v_ref):
    bb, tT, _ = zu_ref.shape
    R = ROW_BLOCK
    for c in range(bb * tT // R):
        if bb == 1:
            zu = zu_ref[0, c * R:(c + 1) * R, :]
            zv = zv_ref[0, c * R:(c + 1) * R, :]
        else:
            zu = zu_ref[...].reshape(R, D_SGU)
            zv = zv_ref[...].reshape(R, D_SGU)
        v = _layer_norm(_gelu_tanh(zv), g_ref[...], b_ref[...])
        vb = v.astype(BF16)
        mixed = jnp.concatenate(
            [_mm(w_ref[g], vb[:, g * SGU_GD:(g + 1) * SGU_GD]) for g in range(SGU_GROUPS)], axis=1)
        out = (_gelu_tanh(zu) * (mixed + bias_ref[...])).astype(y_ref.dtype)
        if bb == 1:
            y_ref[0, c * R:(c + 1) * R, :] = out
        else:
            y_ref[...] = out.reshape(bb, tT, D_SGU)
        for v_ref in maybe_v_ref:
            if bb == 1:
                v_ref[0, c * R:(c + 1) * R, :] = v
            else:
                v_ref[...] = v.reshape(bb, tT, D_SGU)


def _sgu(rows, z, w_blocks, bias_rows, ln_g, ln_b, want_v, name):
    out_specs = [rows.act(D_SGU)]
    out_shape = [jax.ShapeDtypeStruct((rows.nseq, rows.T, D_SGU), BF16)]
    if want_v:
        out_specs.append(rows.act(D_SGU))
        out_shape.append(jax.ShapeDtypeStruct((rows.nseq, rows.T, D_SGU), F32))
    return pl.pallas_call(
        _sgu_kernel,
        grid=rows.grid,
        in_specs=[rows.act(D_SGU, 0), rows.act(D_SGU, 1),
                  _const((SGU_GROUPS, ROW_BLOCK, ROW_BLOCK)), _const((ROW_BLOCK, D_SGU)),
                  _const((1, D_SGU)), _const((1, D_SGU))],
        out_specs=out_specs,
        out_shape=out_shape,
        compiler_params=_params("parallel", "parallel"),
        name=name,
    )(z, z, w_blocks, bias_rows, ln_g.reshape(1, -1), ln_b.reshape(1, -1))


def _inproj_sgu_kernel(x_ref, sh_ref, sc_ref, g_ref, w_ref, ws_ref, bias_ref, lg_ref, lb_ref,
                       z_ref, y_ref, zu_scr, zv_scr, *, col_chunk):
    _, tT, d = x_ref.shape
    n_rest = z_ref.shape[-1]
    R = ROW_BLOCK
    h = _modulated(x_ref[...], g_ref[...], sc_ref[...], sh_ref[...]).reshape(tT, d).astype(BF16)
    for c0 in range(0, D_SGU, col_chunk):
        zv_scr[:, c0:c0 + col_chunk] = _mm(h, w_ref[:, D_SGU + c0:D_SGU + c0 + col_chunk])
    for c0 in range(0, D_SGU, col_chunk):
        zu_scr[:, c0:c0 + col_chunk] = _mm(h, w_ref[:, c0:c0 + col_chunk])

    def project_rest(c0):
        c1 = min(n_rest, c0 + col_chunk)
        z_ref[0, :, c0:c1] = _mm(h, w_ref[:, 2 * D_SGU + c0:2 * D_SGU + c1])

    def gate_rows(r0):
        v = _layer_norm(_gelu_tanh(zv_scr[r0:r0 + R, :]), lg_ref[...], lb_ref[...]).astype(BF16)
        mixed = jnp.concatenate(
            [_mm(ws_ref[g], v[:, g * SGU_GD:(g + 1) * SGU_GD]) for g in range(SGU_GROUPS)], axis=1)
        y_ref[0, r0:r0 + R, :] = (_gelu_tanh(zu_scr[r0:r0 + R, :]) * (mixed + bias_ref[...])).astype(y_ref.dtype)

    rest = list(range(0, n_rest, col_chunk))
    gate = list(range(0, tT, R))
    while rest or gate:
        if rest:
            project_rest(rest.pop(0))
        if gate:
            gate_rows(gate.pop(0))


def _inproj_sgu(rows, x, mods, layer, g, w, w_blocks, bias_rows, ln_g, ln_b, name):
    assert rows.bb == 1 and rows.tT % ROW_BLOCK == 0
    tT = rows.tT
    n_rest = w.shape[1] - 2 * D_SGU
    return pl.pallas_call(
        functools.partial(_inproj_sgu_kernel, col_chunk=512),
        grid=rows.grid,
        in_specs=[rows.act(D_MODEL), rows.mod(layer, 0), rows.mod(layer, 1),
                  _layer_block(layer, (1, 1, D_MODEL)),
                  pl.BlockSpec(w.shape, lambda *_: (0, 0), pipeline_mode=pl.Buffered(1)),
                  _const((SGU_GROUPS, ROW_BLOCK, ROW_BLOCK)), _const((ROW_BLOCK, D_SGU)),
                  _const((1, D_SGU)), _const((1, D_SGU))],
        out_specs=[rows.act(n_rest), rows.act(D_SGU)],
        out_shape=[jax.ShapeDtypeStruct((rows.nseq, rows.T, n_rest), F32),
                   jax.ShapeDtypeStruct((rows.nseq, rows.T, D_SGU), BF16)],
        scratch_shapes=[pltpu.VMEM((tT, D_SGU), F32), pltpu.VMEM((tT, D_SGU), F32)],
        compiler_params=_params("parallel", "parallel"),
        name=name,
    )(x, mods, mods, g.reshape(-1, 1, 1, D_MODEL), w, w_blocks, bias_rows,
      ln_g.reshape(1, -1), ln_b.reshape(1, -1))


def _swa_kernel(sink_ref, q_ref, kc_ref, vc_ref, kp_ref, vp_ref, y_ref, *, prev_from_grid):
    bb, tile_rows, _ = q_ref.shape
    W = WINDOW
    rq = min(tile_rows, W)
    nblk = tile_rows // rq
    assert nblk == 1 or bb == 1
    tq = lax.broadcasted_iota(jnp.int32, (rq, 2 * W), 0)
    sk = lax.broadcasted_iota(jnp.int32, (rq, 2 * W), 1)
    dist_i = tq + W - sk
    in_window = (dist_i >= 0) & (dist_i < W)
    first_key = jnp.where(pl.program_id(1) > 0, 0, W) if prev_from_grid else 0
    dist = dist_i.astype(F32)
    scale = ATT_HD ** -0.5
    for b, j in [(b, j) for b in range(bb) for j in range(nblk)]:
        rows = slice(j * rq, (j + 1) * rq)
        q = q_ref[b, rows, :]
        kc, vc = kc_ref[b, rows, :], vc_ref[b, rows, :]
        if rq < W:
            pad = jnp.zeros((W - rq, KV_WIDTH), F32)
            kc = jnp.concatenate([kc, pad], axis=0)
            vc = jnp.concatenate([vc, pad], axis=0)
        if j == 0:
            kp, vp = kp_ref[b], vp_ref[b]
            valid = in_window & (sk >= first_key)
        else:
            prev_rows = slice((j - 1) * rq, j * rq)
            kp, vp = kc_ref[b, prev_rows, :], vc_ref[b, prev_rows, :]
            valid = in_window
        for g in range(ATT_KV_HEADS):
            gs = slice(g * ATT_HD, (g + 1) * ATT_HD)
            qg = jnp.concatenate(
                [q[:, (g * ATT_REP + r) * ATT_HD:(g * ATT_REP + r + 1) * ATT_HD] for r in range(ATT_REP)],
                axis=0).astype(BF16)
            k_all = jnp.concatenate([kp[:, gs], kc[:, gs]], axis=0).astype(BF16)
            v_all = jnp.concatenate([vp[:, gs], vc[:, gs]], axis=0).astype(BF16)
            s_all = _mm_nt(qg, k_all) * scale
            probs, inv = [], []
            for r in range(ATT_REP):
                h = g * ATT_REP + r
                sink = sink_ref[h]
                s = jnp.where(valid, s_all[r * rq:(r + 1) * rq, :] - ALIBI_SLOPES[h] * dist, NEG_INF)
                m = jnp.maximum(jnp.max(s, axis=-1, keepdims=True), sink)
                p = jnp.exp(s - m)
                inv.append(1.0 / (jnp.sum(p, axis=-1, keepdims=True) + jnp.exp(sink - m)))
                probs.append(p.astype(BF16))
            p_all = jnp.concatenate(probs, axis=0)
            o = _mm(p_all, v_all)
            for r in range(ATT_REP):
                h = g * ATT_REP + r
                y_ref[b, rows, h * ATT_HD:(h + 1) * ATT_HD] = (o[r * rq:(r + 1) * rq, :] * inv[r]).astype(y_ref.dtype)


def _swa(rows, z, col0, sinks, k_prev, v_prev, name):
    bb, tT = rows.bb, rows.tT
    kcol = (col0 + ATT_WIDTH) // KV_WIDTH
    from_grid = k_prev is None
    if from_grid:
        assert tT % WINDOW == 0 and bb == 1
        nw = tT // WINDOW
        kp_spec = pl.BlockSpec((1, WINDOW, KV_WIDTH), lambda i, t: (i, jnp.maximum(t * nw - 1, 0), kcol))
        vp_spec = pl.BlockSpec((1, WINDOW, KV_WIDTH), lambda i, t: (i, jnp.maximum(t * nw - 1, 0), kcol + 1))
        k_prev, v_prev = z, z
    else:
        kp_spec = vp_spec = rows.seq(WINDOW, KV_WIDTH)
    return pl.pallas_call(
        functools.partial(_swa_kernel, prev_from_grid=from_grid),
        grid=rows.grid,
        in_specs=[pl.BlockSpec(memory_space=pltpu.SMEM),
                  rows.act(ATT_WIDTH, col0 // ATT_WIDTH), rows.act(KV_WIDTH, kcol),
                  rows.act(KV_WIDTH, kcol + 1), kp_spec, vp_spec],
        out_specs=rows.act(ATT_WIDTH),
        out_shape=jax.ShapeDtypeStruct((rows.nseq, rows.T, ATT_WIDTH), BF16),
        compiler_params=_params("parallel", "arbitrary"),
        name=name,
    )(sinks, z, z, z, k_prev, v_prev)


def kernel(x_prompt, x_sample, c_prompt, c_sample, state_conv, state_gla, cache_k, cache_v, w_ada, b_ada, g_mix, g_ffn, w_in_ab, conv_w, conv_b, conv_ln_g, conv_ln_b, gla_w_gate, gla_b_gate, gla_g_norm, w_in_cd, sgu_w, sgu_b, sgu_ln_g, sgu_ln_b, att_sinks, w_out, w_ffn_gate, w_ffn_up, w_ffn_down, g_final):
    nb, seq, d = x_prompt.shape
    ndb, dseq, _ = x_sample.shape
    ns = ndb * dseq
    assert d == D_MODEL and dseq <= SAMPLE_ROWS and seq % DENSE_ROWS == 0 and ns % PROJ_ROWS == 0
    sb = ROW_BLOCK // SAMPLE_ROWS
    assert ndb % sb == 0

    w_ab = jnp.pad(w_in_ab, ((0, 0), (0, IN_AB_PAD - IN_AB))).astype(BF16)
    w_cd = w_in_cd.astype(BF16)
    w_o = w_out.astype(BF16)
    w_fg, w_fu, w_fd = w_ffn_gate.astype(BF16), w_ffn_up.astype(BF16), w_ffn_down.astype(BF16)

    tri = jnp.tril(jnp.ones((ROW_BLOCK, ROW_BLOCK), bool))
    sgu_wp = jnp.where(tri, sgu_w, 0.0).astype(BF16)
    sgu_ws = jnp.where(tri, sgu_w, 0.0)[:, :SAMPLE_ROWS, :SAMPLE_ROWS]
    sgu_ws = jnp.einsum("ab,gts->gatbs", jnp.eye(sb, dtype=F32), sgu_ws).reshape(
        SGU_GROUPS, ROW_BLOCK, ROW_BLOCK).astype(BF16)
    sgu_bias_p = jnp.repeat(sgu_b.T, SGU_GD, axis=1)
    sgu_bias_s = jnp.tile(sgu_bias_p[:SAMPLE_ROWS], (sb, 1))

    c_all = jnp.concatenate([jnp.repeat(c_sample, dseq, axis=0), c_prompt], axis=0)
    mods_rows, mods_p = _ada(c_all, ns, w_ada, b_ada)
    mods_s = mods_rows.reshape(mods_rows.shape[0], 1, ns, N_MODS * D_MODEL)

    rp_dense = _Rows(nb, seq, 1, DENSE_ROWS)
    rp_ffn = _Rows(nb, seq, 1, min(seq, FFN_ROWS))
    rs_dense = _Rows(1, ns, 1, min(ns, DENSE_ROWS), per_row=True)
    rp_in = _Rows(nb, seq, 1, PROJ_ROWS)
    rs_in = _Rows(1, ns, 1, PROJ_ROWS, per_row=True)
    rp_mix = _Rows(nb, seq, 1, PROJ_ROWS)
    rp_att = _Rows(nb, seq, 1, WINDOW)
    rs_mix = _Rows(ndb, SAMPLE_ROWS, sb, SAMPLE_ROWS)

    def to_mixer(z):
        z = z.reshape(ndb, dseq, z.shape[-1])
        return jnp.pad(z, ((0, 0), (0, SAMPLE_ROWS - dseq), (0, 0)))

    def to_dense(o):
        return o[:, :dseq].reshape(1, ns, o.shape[-1])

    yp, ys = x_prompt, x_sample.reshape(1, ns, d)

    conv0 = jnp.zeros((nb, CONV_HIST, D_CONV), F32)
    zp, oa_p, st_conv_p = _inproj_conv(rp_in, yp, mods_p, 0, g_mix, w_ab, conv0, conv_w, conv_b,
                                       conv_ln_g, conv_ln_b, "inproj_conv_p")
    zs = to_mixer(_inproj(rs_in, ys, mods_s, 0, g_mix, w_ab, "inproj_ab_s"))
    oa_s, st_conv_s = _conv(rs_mix, zs, state_conv, conv_w, conv_b, conv_ln_g, conv_ln_b, dseq, "conv_s")
    ob_p, st_gla_p = _gla(rp_mix, zp, 0, None, gla_w_gate, gla_b_gate, gla_g_norm, ROW_BLOCK, "gla_p")
    ob_s, st_gla_s = _gla(rs_mix, zs, 2 * D_CONV, state_gla, gla_w_gate, gla_b_gate, gla_g_norm, dseq, "gla_s")
    yp = _outproj(rp_dense, oa_p, ob_p, yp, mods_p, 0, w_o, "outproj0_p")
    ys = _outproj(rs_dense, to_dense(oa_s), to_dense(ob_s), ys, mods_s, 0, w_o, "outproj0_s")
    yp = _ffn(rp_ffn, yp, mods_p, 0, g_ffn, w_fg, w_fu, w_fd, g_final, False, "ffn0_p")
    ys = _ffn(rs_dense, ys, mods_s, 0, g_ffn, w_fg, w_fu, w_fd, g_final, False, "ffn0_s")

    zp, oc_p = _inproj_sgu(rp_in, yp, mods_p, 1, g_mix, w_cd, sgu_wp, sgu_bias_p, sgu_ln_g, sgu_ln_b,
                           "inproj_sgu_p")
    zs = to_mixer(_inproj(rs_in, ys, mods_s, 1, g_mix, w_cd, "inproj_cd_s"))
    oc_s, sgu_v = _sgu(rs_mix, zs, sgu_ws, sgu_bias_s, sgu_ln_g, sgu_ln_b, True, "sgu_s")
    wb = cache_k.shape[1]
    od_p = _swa(rp_att, zp, 0, att_sinks, None, None, "swa_p")
    od_s = _swa(rs_mix, zs, 2 * D_SGU, att_sinks, cache_k.reshape(ndb, wb, KV_WIDTH),
                cache_v.reshape(ndb, wb, KV_WIDTH), "swa_s")
    yp = _outproj(rp_dense, oc_p, od_p, yp, mods_p, 1, w_o, "outproj1_p")
    ys = _outproj(rs_dense, to_dense(oc_s), to_dense(od_s), ys, mods_s, 1, w_o, "outproj1_s")
    yp = _ffn(rp_ffn, yp, mods_p, 1, g_ffn, w_fg, w_fu, w_fd, g_final, True, "ffn1_p")
    ys = _ffn(rs_dense, ys, mods_s, 1, g_ffn, w_fg, w_fu, w_fd, g_final, True, "ffn1_s")

    k0 = 2 * D_SGU + ATT_WIDTH
    kv_shape = (ATT_KV_HEADS, ATT_HD)
    kp0 = ATT_WIDTH
    k_new_p = zp[:, seq - wb:, kp0:kp0 + KV_WIDTH].reshape(nb, wb, *kv_shape)
    v_new_p = zp[:, seq - wb:, kp0 + KV_WIDTH:kp0 + 2 * KV_WIDTH].reshape(nb, wb, *kv_shape)
    k_new_s = zs[:, :dseq, k0:k0 + KV_WIDTH].reshape(ndb, dseq, *kv_shape)
    v_new_s = zs[:, :dseq, k0 + KV_WIDTH:k0 + 2 * KV_WIDTH].reshape(ndb, dseq, *kv_shape)
    assert dseq <= wb
    k_s = jnp.concatenate([cache_k[:, dseq:], k_new_s], axis=1)
    v_s = jnp.concatenate([cache_v[:, dseq:], v_new_s], axis=1)
    return (yp, ys.reshape(ndb, dseq, d), st_conv_p, st_conv_s, st_gla_p, st_gla_s, sgu_v[:, :dseq],
            k_new_p, k_s, v_new_p, v_s)
```

```python
import functools

import jax
import jax.numpy as jnp
from jax import lax
from jax.experimental import pallas as pl
from jax.experimental.pallas import tpu as pltpu

F32 = jnp.float32
BF16 = jnp.bfloat16

D_MODEL = 2048
NORM_EPS = 1e-6
NEG_INF = -1e30
D_CONV = 1024
CONV_WIDTH = 31
CONV_HIST = CONV_WIDTH - 1
GLA_HEADS = 4
GLA_DV = 1024
GLA_DK = 512
GLA_HK = 128
GLA_HV = 256
GLA_RANK = 16
GLA_GATE_NORM = 16.0
D_SGU = 1024
SGU_GROUPS = 4
SGU_GD = 256
ATT_HD = 64
ATT_HEADS = 16
ATT_KV_HEADS = 2
ATT_REP = 8
ATT_WIDTH = 1024
KV_WIDTH = 128
WINDOW = 128
D_FF = 5632
IN_AB = 5136
IN_AB_PAD = 5248
N_MODS = 6

LANES = 128
SUBLANES = 8
SAMPLE_ROWS = SUBLANES
ROW_BLOCK = 128
PROJ_ROWS = 256
DENSE_ROWS = 512
FFN_CHUNK = 512
FFN_ROWS = 1024
V7X_VMEM_BYTES = 64 * 1024 * 1024
FFN_VMEM_LIMIT = V7X_VMEM_BYTES - 1024 * 1024
VMEM_LIMIT = V7X_VMEM_BYTES - 8 * 1024 * 1024

ALIBI_SLOPES = tuple(2.0 ** (-8.0 * (h + 1) / ATT_HEADS) for h in range(ATT_HEADS))


def _mm(a, b):
    return jnp.dot(a, b, preferred_element_type=F32)


def _mm_nt(a, b):
    return lax.dot_general(a, b, (((1,), (1,)), ((), ())), preferred_element_type=F32)


def _sigmoid(x):
    return 1.0 / (1.0 + jnp.exp(-x))


def _silu(x):
    return x * _sigmoid(x)


def _gelu_tanh(x):
    c = 0.7978845608028654
    return 0.5 * x * (1.0 + jnp.tanh(c * (x + 0.044715 * (x * x * x))))


def _log_sigmoid(x):
    return jnp.minimum(x, 0.0) - jnp.log(1.0 + jnp.exp(-jnp.abs(x)))


def _rms(x):
    return x * lax.rsqrt(jnp.mean(x * x, axis=-1, keepdims=True) + NORM_EPS)


def _layer_norm(x, g, b):
    mu = jnp.mean(x, axis=-1, keepdims=True)
    xc = x - mu
    return xc * lax.rsqrt(jnp.mean(xc * xc, axis=-1, keepdims=True) + NORM_EPS) * g + b


def _params(*sem):
    return pltpu.CompilerParams(dimension_semantics=sem, vmem_limit_bytes=VMEM_LIMIT)


def _const(shape):
    nd = len(shape)
    return pl.BlockSpec(shape, lambda *_: (0,) * nd)


def _layer_block(layer, tail, resident=False):
    nz = (0,) * len(tail)
    mode = dict(pipeline_mode=pl.Buffered(1)) if resident else {}
    return pl.BlockSpec((None,) + tuple(tail), lambda *_: (layer,) + nz, **mode)


def _ada_kernel(c_ref, w_ref, b_ref, rows_ref, seq_ref, a_scr):
    @pl.when((pl.program_id(0) == 0) & (pl.program_id(1) == 0))
    def _():
        a_scr[...] = _silu(c_ref[...]).astype(BF16)

    res = _mm(a_scr[...], w_ref[0].astype(BF16)) + b_ref[0]
    ns = rows_ref.shape[1]
    rows_ref[0] = res[:ns]
    for r in range(seq_ref.shape[1]):
        seq_ref[0, r] = res[ns + r:ns + r + 1]


def _ada(c_all, n_rows, w_ada, b_ada, tn=1024):
    depth, d, n = w_ada.shape
    rows = c_all.shape[0]
    nb = rows - n_rows
    return pl.pallas_call(
        _ada_kernel,
        grid=(depth, n // tn),
        in_specs=[pl.BlockSpec((rows, d), lambda l, j: (0, 0)),
                  pl.BlockSpec((1, d, tn), lambda l, j: (l, 0, j)),
                  pl.BlockSpec((1, 1, tn), lambda l, j: (l, 0, j))],
        out_specs=[pl.BlockSpec((1, n_rows, tn), lambda l, j: (l, 0, j)),
                   pl.BlockSpec((1, nb, 1, tn), lambda l, j: (l, 0, 0, j))],
        out_shape=[jax.ShapeDtypeStruct((depth, n_rows, n), F32),
                   jax.ShapeDtypeStruct((depth, nb, 1, n), F32)],
        scratch_shapes=[pltpu.VMEM((rows, d), BF16)],
        compiler_params=_params("arbitrary", "arbitrary"),
        name="ada",
    )(c_all, w_ada, b_ada.reshape(depth, 1, n))


class _Rows:
    def __init__(self, nseq, T, bb, tT, per_row=False):
        assert nseq % bb == 0 and T % tT == 0 and (nseq == 1 or not per_row)
        self.nseq, self.T, self.bb, self.tT = nseq, T, bb, tT
        self.nt = T // tT
        self.grid = (nseq // bb, self.nt)
        self.ntiles = self.grid[0] * self.nt
        self.per_row = per_row

    def _tile(self, tile):
        nt = self.nt
        return tile if tile is not None else (lambda i, t, *_: i * nt + t)

    def act(self, width, col=0, tile=None):
        nt, fn = self.nt, self._tile(tile)
        return pl.BlockSpec((self.bb, self.tT, width), lambda *g: (fn(*g) // nt, fn(*g) % nt, col))

    def mod(self, layer, k, tile=None):
        nt, fn = self.nt, self._tile(tile)
        if self.per_row:
            return pl.BlockSpec((None, 1, self.tT, D_MODEL), lambda *g: (layer, 0, fn(*g) % nt, k))
        return pl.BlockSpec((None, self.bb, 1, D_MODEL), lambda *g: (layer, fn(*g) // nt, 0, k))

    def seq(self, *tail):
        nz = (0,) * len(tail)
        return pl.BlockSpec((self.bb,) + tuple(tail), lambda i, t, *_: (i,) + nz)


def _modulated(x, g, scale, shift):
    return (_rms(x) * g) * (1.0 + scale) + shift


def _inproj_kernel(x_ref, sh_ref, sc_ref, g_ref, w_ref, z_ref, *, col_chunk):
    bb, tT, d = x_ref.shape
    h = _modulated(x_ref[...], g_ref[...], sc_ref[...], sh_ref[...])
    h = h.reshape(bb * tT, d).astype(BF16)
    n = w_ref.shape[1]
    for c0 in range(0, n, col_chunk):
        c1 = min(n, c0 + col_chunk)
        z_ref[:, :, c0:c1] = _mm(h, w_ref[:, c0:c1]).reshape(bb, tT, c1 - c0)


def _inproj(rows, x, mods, layer, g, w, name):
    n = w.shape[1]
    return pl.pallas_call(
        functools.partial(_inproj_kernel, col_chunk=512),
        grid=rows.grid,
        in_specs=[rows.act(D_MODEL), rows.mod(layer, 0), rows.mod(layer, 1),
                  _layer_block(layer, (1, 1, D_MODEL)),
                  pl.BlockSpec((D_MODEL, n), lambda *_: (0, 0), pipeline_mode=pl.Buffered(1))],
        out_specs=rows.act(n),
        out_shape=jax.ShapeDtypeStruct((rows.nseq, rows.T, n), F32),
        compiler_params=_params("parallel", "parallel"),
        name=name,
    )(x, mods, mods, g.reshape(-1, 1, 1, D_MODEL), w)


def _outproj_kernel(oa_ref, ob_ref, y_ref, gate_ref, w_ref, o_ref):
    bb, tT, d = y_ref.shape
    half = oa_ref.shape[-1]
    oa = oa_ref[...].reshape(bb * tT, half)
    ob = ob_ref[...].reshape(bb * tT, half)
    acc = _mm(oa, w_ref[:half, :]) + _mm(ob, w_ref[half:, :])
    o_ref[...] = y_ref[...] + gate_ref[...] * acc.reshape(bb, tT, d)


def _outproj(rows, oa, ob, y, mods, layer, w, name):
    half = oa.shape[-1]
    return pl.pallas_call(
        _outproj_kernel,
        grid=rows.grid,
        in_specs=[rows.act(half), rows.act(half), rows.act(D_MODEL), rows.mod(layer, 2),
                  _layer_block(layer, (D_MODEL, D_MODEL), resident=True)],
        out_specs=rows.act(D_MODEL),
        out_shape=jax.ShapeDtypeStruct(y.shape, F32),
        compiler_params=_params("parallel", "parallel"),
        name=name,
    )(oa, ob, y, mods, w)


def _ffn_kernel(y_ref, sh_ref, sc_ref, gt_ref, g_ref, wg_ref, wu_ref, wd_ref, gf_ref, o_ref,
                h_scr, *, final):
    f = pl.program_id(2)
    bb, tT, d = y_ref.shape

    @pl.when(f == 0)
    def _():
        h = _modulated(y_ref[...], g_ref[...], sc_ref[...], sh_ref[...])
        h_scr[...] = h.reshape(bb * tT, d).astype(BF16)
        o_ref[...] = jnp.zeros_like(o_ref)

    h = h_scr[...]
    a = _mm(h, wg_ref[...])
    b = _mm(h, wu_ref[...])
    p = (_silu(a) * b).astype(BF16)
    o_ref[...] += _mm(p, wd_ref[...]).reshape(bb, tT, d)

    @pl.when(f == pl.num_programs(2) - 1)
    def _():
        out = y_ref[...] + gt_ref[...] * o_ref[...]
        if final:
            out = _rms(out) * gf_ref[...]
        o_ref[...] = out


def _ffn(rows, y, mods, layer, g, wg, wu, wd, g_final, final, name, tf=FFN_CHUNK):
    nf = D_FF // tf
    m = rows.bb * rows.tT
    return pl.pallas_call(
        functools.partial(_ffn_kernel, final=final),
        grid=rows.grid + (nf,),
        in_specs=[rows.act(D_MODEL), rows.mod(layer, 3), rows.mod(layer, 4), rows.mod(layer, 5),
                  _layer_block(layer, (1, 1, D_MODEL)),
                  pl.BlockSpec((None, D_MODEL, tf), lambda i, t, f: (layer, 0, f)),
                  pl.BlockSpec((None, D_MODEL, tf), lambda i, t, f: (layer, 0, f)),
                  pl.BlockSpec((None, tf, D_MODEL), lambda i, t, f: (layer, f, 0)),
                  _const((1, 1, D_MODEL))],
        out_specs=rows.act(D_MODEL),
        out_shape=jax.ShapeDtypeStruct(y.shape, F32),
        scratch_shapes=[pltpu.VMEM((m, D_MODEL), BF16)],
        compiler_params=pltpu.CompilerParams(dimension_semantics=("parallel", "parallel", "arbitrary"),
                                             vmem_limit_bytes=FFN_VMEM_LIMIT),
        name=name,
    )(y, mods, mods, mods, g.reshape(-1, 1, 1, D_MODEL), wg, wu, wd, g_final.reshape(1, 1, D_MODEL))


CONV_PAD = 32


def _conv_kernel(za_ref, prev_ref, w_ref, cb_ref, g_ref, b_ref, y_ref, st_ref, ubuf, ybuf,
                 *, t_valid, rc, lc):
    t = pl.program_id(1)
    bb, tT, _ = za_ref.shape
    lo = CONV_PAD - CONV_HIST
    assert lc == LANES
    nlc = D_CONV // lc

    @pl.when(t == 0)
    def _():
        for b in range(bb):
            for ci in range(nlc):
                ubuf[b * nlc + ci, lo:CONV_PAD, :] = prev_ref[b, :, ci * lc:(ci + 1) * lc]

    @pl.when(t > 0)
    def _():
        ubuf[:, lo:CONV_PAD, :] = ubuf[:, tT + lo:tT + CONV_PAD, :]

    for b in range(bb):
        for ci in range(nlc):
            c0 = ci * lc
            ubuf[b * nlc + ci, CONV_PAD:CONV_PAD + tT, :] = (
                za_ref[b, :, c0:c0 + lc] * _sigmoid(za_ref[b, :, D_CONV + c0:D_CONV + c0 + lc]))

    for b in range(bb):
        def row_chunk(i, carry, b=b):
            r0 = pl.multiple_of(i * rc, SUBLANES)
            for ci in range(nlc):
                c0 = ci * lc
                acc = jnp.broadcast_to(cb_ref[:, c0:c0 + lc], (rc, lc))
                for j in range(CONV_WIDTH):
                    acc = acc + w_ref[j:j + 1, c0:c0 + lc] * ubuf[b * nlc + ci, pl.ds(r0 + lo + j, rc), :]
                ybuf[b, pl.ds(r0, rc), c0:c0 + lc] = acc
            return carry
        lax.fori_loop(0, tT // rc, row_chunk, 0)

    y = _layer_norm(ybuf[...], g_ref[...], b_ref[...])
    y_ref[...] = _silu(y).astype(y_ref.dtype)

    @pl.when(t == pl.num_programs(1) - 1)
    def _():
        for b in range(bb):
            for ci in range(nlc):
                st_ref[b, :, ci * lc:(ci + 1) * lc] = ubuf[b * nlc + ci, t_valid + lo:t_valid + CONV_PAD, :]


def _conv(rows, z, prev, conv_w, conv_b, ln_g, ln_b, t_valid, name):
    bb, tT = rows.bb, rows.tT
    rc = min(tT, 64)
    return pl.pallas_call(
        functools.partial(_conv_kernel, t_valid=t_valid, rc=rc, lc=LANES),
        grid=rows.grid,
        in_specs=[rows.act(2 * D_CONV, 0), rows.seq(CONV_HIST, D_CONV),
                  _const((CONV_WIDTH, D_CONV)), _const((1, D_CONV)), _const((1, D_CONV)), _const((1, D_CONV))],
        out_specs=[rows.act(D_CONV), rows.seq(CONV_HIST, D_CONV)],
        out_shape=[jax.ShapeDtypeStruct((rows.nseq, rows.T, D_CONV), BF16),
                   jax.ShapeDtypeStruct((rows.nseq, CONV_HIST, D_CONV), F32)],
        scratch_shapes=[pltpu.VMEM((bb * (D_CONV // LANES), CONV_PAD + tT, LANES), F32),
                        pltpu.VMEM((bb, tT, D_CONV), F32)],
        compiler_params=_params("parallel", "arbitrary"),
        name=name,
    )(z, prev, conv_w, conv_b.reshape(1, -1), ln_g.reshape(1, -1), ln_b.reshape(1, -1))


def _inproj_conv_kernel(x_ref, sh_ref, sc_ref, g_ref, w_ref, prev_ref, cw_ref, cb_ref, lg_ref, lb_ref,
                        z_ref, y_ref, st_ref, ubuf, ybuf, *, rc, col_chunk):
    t = pl.program_id(1)
    _, tT, d = x_ref.shape
    lo = CONV_PAD - CONV_HIST
    lc = LANES
    nlc = D_CONV // lc
    n_rest = z_ref.shape[-1]

    @pl.when(t == 0)
    def _():
        for ci in range(nlc):
            ubuf[ci, lo:CONV_PAD, :] = prev_ref[0, :, ci * lc:(ci + 1) * lc]

    @pl.when(t > 0)
    def _():
        ubuf[:, lo:CONV_PAD, :] = ubuf[:, tT + lo:tT + CONV_PAD, :]

    h = _modulated(x_ref[...], g_ref[...], sc_ref[...], sh_ref[...]).reshape(tT, d).astype(BF16)

    for c0 in range(0, D_CONV, col_chunk):
        u = _mm(h, w_ref[:, c0:c0 + col_chunk]) * _sigmoid(_mm(h, w_ref[:, D_CONV + c0:D_CONV + c0 + col_chunk]))
        for k in range(col_chunk // lc):
            ubuf[c0 // lc + k, CONV_PAD:CONV_PAD + tT, :] = u[:, k * lc:(k + 1) * lc]

    def project_rest(c0):
        c1 = min(n_rest, c0 + col_chunk)
        z_ref[0, :, c0:c1] = _mm(h, w_ref[:, 2 * D_CONV + c0:2 * D_CONV + c1])

    def conv_rows(r0):
        for ci in range(nlc):
            c0 = ci * lc
            acc = jnp.broadcast_to(cb_ref[:, c0:c0 + lc], (rc, lc))
            for j in range(CONV_WIDTH):
                acc = acc + cw_ref[j:j + 1, c0:c0 + lc] * ubuf[ci, r0 + lo + j:r0 + lo + j + rc, :]
            ybuf[r0:r0 + rc, c0:c0 + lc] = acc
        y = _layer_norm(ybuf[r0:r0 + rc, :], lg_ref[...], lb_ref[...])
        y_ref[0, r0:r0 + rc, :] = _silu(y).astype(y_ref.dtype)

    rest = list(range(0, n_rest, col_chunk))
    conv = list(range(0, tT, rc))
    while rest or conv:
        if rest:
            project_rest(rest.pop(0))
        if conv:
            conv_rows(conv.pop(0))

    @pl.when(t == pl.num_programs(1) - 1)
    def _():
        for ci in range(nlc):
            st_ref[0, :, ci * lc:(ci + 1) * lc] = ubuf[ci, tT + lo:tT + CONV_PAD, :]


def _inproj_conv(rows, x, mods, layer, g, w, prev, conv_w, conv_b, ln_g, ln_b, name):
    assert rows.bb == 1
    tT = rows.tT
    n_rest = w.shape[1] - 2 * D_CONV
    return pl.pallas_call(
        functools.partial(_inproj_conv_kernel, rc=32, col_chunk=512),
        grid=rows.grid,
        in_specs=[rows.act(D_MODEL), rows.mod(layer, 0), rows.mod(layer, 1),
                  _layer_block(layer, (1, 1, D_MODEL)),
                  pl.BlockSpec(w.shape, lambda *_: (0, 0), pipeline_mode=pl.Buffered(1)),
                  rows.seq(CONV_HIST, D_CONV),
                  _const((CONV_WIDTH, D_CONV)), _const((1, D_CONV)), _const((1, D_CONV)), _const((1, D_CONV))],
        out_specs=[rows.act(n_rest), rows.act(D_CONV), rows.seq(CONV_HIST, D_CONV)],
        out_shape=[jax.ShapeDtypeStruct((rows.nseq, rows.T, n_rest), F32),
                   jax.ShapeDtypeStruct((rows.nseq, rows.T, D_CONV), BF16),
                   jax.ShapeDtypeStruct((rows.nseq, CONV_HIST, D_CONV), F32)],
        scratch_shapes=[pltpu.VMEM((D_CONV // LANES, CONV_PAD + tT, LANES), F32),
                        pltpu.VMEM((tT, D_CONV), F32)],
        compiler_params=_params("parallel", "arbitrary"),
        name=name,
    )(x, mods, mods, g.reshape(-1, 1, 1, D_MODEL), w, prev, conv_w, conv_b.reshape(1, -1),
      ln_g.reshape(1, -1), ln_b.reshape(1, -1))


def _segment_cumsum(x, seg):
    pos = lax.broadcasted_iota(jnp.int32, x.shape, 0) % seg
    s = 1
    while s < seg:
        x = x + jnp.where(pos >= s, pltpu.roll(x, s, 0), 0.0)
        s *= 2
    return x


def _gla_kernel(*refs, nseg, seg, t_valid, has_state):
    if has_state:
        (zq_ref, zk_ref, zv_ref, zg_ref, zr_ref, wg_ref, bg_ref, gn_ref, s0_ref,
         y_ref, st_ref, st_scr) = refs
    else:
        (zq_ref, zk_ref, zv_ref, zg_ref, zr_ref, wg_ref, bg_ref, gn_ref,
         y_ref, st_ref, st_scr) = refs
    t = pl.program_id(1)
    bb, tT, _ = zq_ref.shape
    R = ROW_BLOCK
    assert nseg * seg == R and (bb * tT) % R == 0

    @pl.when(t == 0)
    def _():
        for g in range(nseg):
            for h in range(GLA_HEADS):
                st_scr[g, h] = s0_ref[g, h].T if has_state else jnp.zeros((GLA_HV, GLA_HK), F32)

    def rows_of(ref, c):
        if nseg == 1:
            return ref[0, c * R:(c + 1) * R, :]
        return ref[...].reshape(R, ref.shape[-1])

    row = lax.broadcasted_iota(jnp.int32, (R, R), 0)
    col = lax.broadcasted_iota(jnp.int32, (R, R), 1)
    causal = (row >= col) & ((row // seg) == (col // seg))
    pos = lax.broadcasted_iota(jnp.int32, (R, GLA_DK), 0) % seg
    colseg = lax.broadcasted_iota(jnp.int32, (GLA_HV, R), 1) // seg

    for c in range(bb * tT // R):
        zr = rows_of(zr_ref, c).astype(BF16)
        la = _log_sigmoid(_mm(zr, wg_ref[...]) + bg_ref[...]) * (1.0 / GLA_GATE_NORM)
        q = rows_of(zq_ref, c) * (GLA_HK ** -0.5)
        k = rows_of(zk_ref, c)
        v = rows_of(zv_ref, c)
        zg = rows_of(zg_ref, c)
        if t_valid < seg:
            la = jnp.where(pos < t_valid, la, 0.0)
            k = jnp.where(pos < t_valid, k, 0.0)
        bcum = _segment_cumsum(la, seg)
        b3 = bcum.reshape(nseg, seg, GLA_DK)
        bl3 = b3[:, seg - 1:seg, :]
        k_last = k * jnp.exp(bl3 - b3).reshape(R, GLA_DK)
        dec3 = jnp.exp(bl3)
        q_dec = q * jnp.exp(bcum)
        bmid = b3[:, seg // 2 - 1:seg // 2, :] if nseg == 1 else jnp.zeros_like(bl3)
        q_in = (q * jnp.exp(b3 - bmid).reshape(R, GLA_DK)).astype(BF16)
        k_in = (k * jnp.exp(bmid - b3).reshape(R, GLA_DK)).astype(BF16)
        q_dec = q_dec.astype(BF16)
        k_last = k_last.astype(BF16)
        for h in range(GLA_HEADS):
            ks = slice(h * GLA_HK, (h + 1) * GLA_HK)
            vs = slice(h * GLA_HV, (h + 1) * GLA_HV)
            v_h = v[:, vs]
            att = jnp.where(causal, _mm_nt(q_in[:, ks], k_in[:, ks]), 0.0)
            o = _mm(att.astype(BF16), v_h.astype(BF16))
            vt = v_h.T
            inter = []
            for g in range(nseg):
                st = st_scr[g, h]
                inter.append(_mm_nt(q_dec[g * seg:(g + 1) * seg, ks], st.astype(BF16)))
                vt_g = vt if nseg == 1 else jnp.where(colseg == g, vt, 0.0)
                st_scr[g, h] = st * dec3[g, :, ks] + _mm(vt_g.astype(BF16), k_last[:, ks])
            o = o + (inter[0] if nseg == 1 else jnp.concatenate(inter, axis=0))
            o = _rms(o) * gn_ref[...] * _silu(zg[:, vs])
            o = o.astype(y_ref.dtype)
            if nseg == 1:
                y_ref[0, c * R:(c + 1) * R, vs] = o
            else:
                y_ref[:, :, vs] = o.reshape(bb, tT, GLA_HV)

    @pl.when(t == pl.num_programs(1) - 1)
    def _():
        for g in range(nseg):
            for h in range(GLA_HEADS):
                st_ref[g, h] = st_scr[g, h].T


def _gla(rows, z, col0, s0, w_gate, b_gate, g_norm, t_valid, name):
    bb = rows.bb
    nseg = bb if bb > 1 else 1
    seg = ROW_BLOCK // nseg
    has_state = s0 is not None
    wg = jnp.pad(w_gate, ((0, LANES - GLA_RANK), (0, 0))).astype(BF16)
    state_spec = rows.seq(GLA_HEADS, GLA_HK, GLA_HV)
    in_specs = [rows.act(GLA_DK, col0 // GLA_DK), rows.act(GLA_DK, col0 // GLA_DK + 1),
                rows.act(GLA_DV, (col0 + 2 * GLA_DK) // GLA_DV),
                rows.act(GLA_DV, (col0 + 2 * GLA_DK) // GLA_DV + 1),
                rows.act(LANES, (col0 + 2 * GLA_DK + 2 * GLA_DV) // LANES),
                _const((LANES, GLA_DK)), _const((1, GLA_DK)), _const((1, GLA_HV))]
    args = [z, z, z, z, z, wg, b_gate.reshape(1, -1), g_norm.reshape(1, -1)]
    if has_state:
        in_specs.append(state_spec)
        args.append(s0)
    return pl.pallas_call(
        functools.partial(_gla_kernel, nseg=nseg, seg=seg, t_valid=t_valid, has_state=has_state),
        grid=rows.grid,
        in_specs=in_specs,
        out_specs=[rows.act(GLA_DV), state_spec],
        out_shape=[jax.ShapeDtypeStruct((rows.nseq, rows.T, GLA_DV), BF16),
                   jax.ShapeDtypeStruct((rows.nseq, GLA_HEADS, GLA_HK, GLA_HV), F32)],
        scratch_shapes=[pltpu.VMEM((nseg, GLA_HEADS, GLA_HV, GLA_HK), F32)],
        compiler_params=_params("parallel", "arbitrary"),
        name=name,
    )(*args)


def _sgu_kernel(zu_ref, zv_ref, w_ref, bias_ref, g_ref, b_ref, y_ref, *maybe_v_ref):
    bb, tT, _ = zu_ref.shape
    R = ROW_BLOCK
    for c in range(bb * tT // R):
        if bb == 1:
            zu = zu_ref[0, c * R:(c + 1) * R, :]
            zv = zv_ref[0, c * R:(c + 1) * R, :]
        else:
            zu = zu_ref[...].reshape(R, D_SGU)
            zv = zv_ref[...].reshape(R, D_SGU)
        v = _layer_norm(_gelu_tanh(zv), g_ref[...], b_ref[...])
        vb = v.astype(BF16)
        mixed = jnp.concatenate(
            [_mm(w_ref[g], vb[:, g * SGU_GD:(g + 1) * SGU_GD]) for g in range(SGU_GROUPS)], axis=1)
        out = (_gelu_tanh(zu) * (mixed + bias_ref[...])).astype(y_ref.dtype)
        if bb == 1:
            y_ref[0, c * R:(c + 1) * R, :] = out
        else:
            y_ref[...] = out.reshape(bb, tT, D_SGU)
        for v_ref in maybe_v_ref:
            if bb == 1:
                v_ref[0, c * R:(c + 1) * R, :] = v
            else:
                v_ref[...] = v.reshape(bb, tT, D_SGU)


def _sgu(rows, z, w_blocks, bias_rows, ln_g, ln_b, want_v, name):
    out_specs = [rows.act(D_SGU)]
    out_shape = [jax.ShapeDtypeStruct((rows.nseq, rows.T, D_SGU), BF16)]
    if want_v:
        out_specs.append(rows.act(D_SGU))
        out_shape.append(jax.ShapeDtypeStruct((rows.nseq, rows.T, D_SGU), F32))
    return pl.pallas_call(
        _sgu_kernel,
        grid=rows.grid,
        in_specs=[rows.act(D_SGU, 0), rows.act(D_SGU, 1),
                  _const((SGU_GROUPS, ROW_BLOCK, ROW_BLOCK)), _const((ROW_BLOCK, D_SGU)),
                  _const((1, D_SGU)), _const((1, D_SGU))],
        out_specs=out_specs,
        out_shape=out_shape,
        compiler_params=_params("parallel", "parallel"),
        name=name,
    )(z, z, w_blocks, bias_rows, ln_g.reshape(1, -1), ln_b.reshape(1, -1))


def _inproj_sgu_kernel(x_ref, sh_ref, sc_ref, g_ref, w_ref, ws_ref, bias_ref, lg_ref, lb_ref,
                       z_ref, y_ref, zu_scr, zv_scr, *, col_chunk):
    _, tT, d = x_ref.shape
    n_rest = z_ref.shape[-1]
    R = ROW_BLOCK
    h = _modulated(x_ref[...], g_ref[...], sc_ref[...], sh_ref[...]).reshape(tT, d).astype(BF16)
    for c0 in range(0, D_SGU, col_chunk):
        zv_scr[:, c0:c0 + col_chunk] = _mm(h, w_ref[:, D_SGU + c0:D_SGU + c0 + col_chunk])
    for c0 in range(0, D_SGU, col_chunk):
        zu_scr[:, c0:c0 + col_chunk] = _mm(h, w_ref[:, c0:c0 + col_chunk])

    def project_rest(c0):
        c1 = min(n_rest, c0 + col_chunk)
        z_ref[0, :, c0:c1] = _mm(h, w_ref[:, 2 * D_SGU + c0:2 * D_SGU + c1])

    def gate_rows(r0):
        v = _layer_norm(_gelu_tanh(zv_scr[r0:r0 + R, :]), lg_ref[...], lb_ref[...]).astype(BF16)
        mixed = jnp.concatenate(
            [_mm(ws_ref[g], v[:, g * SGU_GD:(g + 1) * SGU_GD]) for g in range(SGU_GROUPS)], axis=1)
        y_ref[0, r0:r0 + R, :] = (_gelu_tanh(zu_scr[r0:r0 + R, :]) * (mixed + bias_ref[...])).astype(y_ref.dtype)

    rest = list(range(0, n_rest, col_chunk))
    gate = list(range(0, tT, R))
    while rest or gate:
        if rest:
            project_rest(rest.pop(0))
        if gate:
            gate_rows(gate.pop(0))


def _inproj_sgu(rows, x, mods, layer, g, w, w_blocks, bias_rows, ln_g, ln_b, name):
    assert rows.bb == 1 and rows.tT % ROW_BLOCK == 0
    tT = rows.tT
    n_rest = w.shape[1] - 2 * D_SGU
    return pl.pallas_call(
        functools.partial(_inproj_sgu_kernel, col_chunk=512),
        grid=rows.grid,
        in_specs=[rows.act(D_MODEL), rows.mod(layer, 0), rows.mod(layer, 1),
                  _layer_block(layer, (1, 1, D_MODEL)),
                  pl.BlockSpec(w.shape, lambda *_: (0, 0), pipeline_mode=pl.Buffered(1)),
                  _const((SGU_GROUPS, ROW_BLOCK, ROW_BLOCK)), _const((ROW_BLOCK, D_SGU)),
                  _const((1, D_SGU)), _const((1, D_SGU))],
        out_specs=[rows.act(n_rest), rows.act(D_SGU)],
        out_shape=[jax.ShapeDtypeStruct((rows.nseq, rows.T, n_rest), F32),
                   jax.ShapeDtypeStruct((rows.nseq, rows.T, D_SGU), BF16)],
        scratch_shapes=[pltpu.VMEM((tT, D_SGU), F32), pltpu.VMEM((tT, D_SGU), F32)],
        compiler_params=_params("parallel", "parallel"),
        name=name,
    )(x, mods, mods, g.reshape(-1, 1, 1, D_MODEL), w, w_blocks, bias_rows,
      ln_g.reshape(1, -1), ln_b.reshape(1, -1))


def _swa_kernel(sink_ref, q_ref, kc_ref, vc_ref, kp_ref, vp_ref, y_ref, *, prev_from_grid):
    bb, tile_rows, _ = q_ref.shape
    W = WINDOW
    rq = min(tile_rows, W)
    nblk = tile_rows // rq
    assert nblk == 1 or bb == 1
    tq = lax.broadcasted_iota(jnp.int32, (rq, 2 * W), 0)
    sk = lax.broadcasted_iota(jnp.int32, (rq, 2 * W), 1)
    dist_i = tq + W - sk
    in_window = (dist_i >= 0) & (dist_i < W)
    first_key = jnp.where(pl.program_id(1) > 0, 0, W) if prev_from_grid else 0
    dist = dist_i.astype(F32)
    scale = ATT_HD ** -0.5
    for b, j in [(b, j) for b in range(bb) for j in range(nblk)]:
        rows = slice(j * rq, (j + 1) * rq)
        q = q_ref[b, rows, :]
        kc, vc = kc_ref[b, rows, :], vc_ref[b, rows, :]
        if rq < W:
            pad = jnp.zeros((W - rq, KV_WIDTH), F32)
            kc = jnp.concatenate([kc, pad], axis=0)
            vc = jnp.concatenate([vc, pad], axis=0)
        if j == 0:
            kp, vp = kp_ref[b], vp_ref[b]
            valid = in_window & (sk >= first_key)
        else:
            prev_rows = slice((j - 1) * rq, j * rq)
            kp, vp = kc_ref[b, prev_rows, :], vc_ref[b, prev_rows, :]
            valid = in_window
        for g in range(ATT_KV_HEADS):
            gs = slice(g * ATT_HD, (g + 1) * ATT_HD)
            qg = jnp.concatenate(
                [q[:, (g * ATT_REP + r) * ATT_HD:(g * ATT_REP + r + 1) * ATT_HD] for r in range(ATT_REP)],
                axis=0).astype(BF16)
            k_all = jnp.concatenate([kp[:, gs], kc[:, gs]], axis=0).astype(BF16)
            v_all = jnp.concatenate([vp[:, gs], vc[:, gs]], axis=0).astype(BF16)
            s_all = _mm_nt(qg, k_all) * scale
            probs, inv = [], []
            for r in range(ATT_REP):
                h = g * ATT_REP + r
                sink = sink_ref[h]
                s = jnp.where(valid, s_all[r * rq:(r + 1) * rq, :] - ALIBI_SLOPES[h] * dist, NEG_INF)
                m = jnp.maximum(jnp.max(s, axis=-1, keepdims=True), sink)
                p = jnp.exp(s - m)
                inv.append(1.0 / (jnp.sum(p, axis=-1, keepdims=True) + jnp.exp(sink - m)))
                probs.append(p.astype(BF16))
            p_all = jnp.concatenate(probs, axis=0)
            o = _mm(p_all, v_all)
            for r in range(ATT_REP):
                h = g * ATT_REP + r
                y_ref[b, rows, h * ATT_HD:(h + 1) * ATT_HD] = (o[r * rq:(r + 1) * rq, :] * inv[r]).astype(y_ref.dtype)


def _swa(rows, z, col0, sinks, k_prev, v_prev, name):
    bb, tT = rows.bb, rows.tT
    kcol = (col0 + ATT_WIDTH) // KV_WIDTH
    from_grid = k_prev is None
    if from_grid:
        assert tT % WINDOW == 0 and bb == 1
        nw = tT // WINDOW
        kp_spec = pl.BlockSpec((1, WINDOW, KV_WIDTH), lambda i, t: (i, jnp.maximum(t * nw - 1, 0), kcol))
        vp_spec = pl.BlockSpec((1, WINDOW, KV_WIDTH), lambda i, t: (i, jnp.maximum(t * nw - 1, 0), kcol + 1))
        k_prev, v_prev = z, z
    else:
        kp_spec = vp_spec = rows.seq(WINDOW, KV_WIDTH)
    return pl.pallas_call(
        functools.partial(_swa_kernel, prev_from_grid=from_grid),
        grid=rows.grid,
        in_specs=[pl.BlockSpec(memory_space=pltpu.SMEM),
                  rows.act(ATT_WIDTH, col0 // ATT_WIDTH), rows.act(KV_WIDTH, kcol),
                  rows.act(KV_WIDTH, kcol + 1), kp_spec, vp_spec],
        out_specs=rows.act(ATT_WIDTH),
        out_shape=jax.ShapeDtypeStruct((rows.nseq, rows.T, ATT_WIDTH), BF16),
        compiler_params=_params("parallel", "arbitrary"),
        name=name,
    )(sinks, z, z, z, k_prev, v_prev)


def kernel(x_prompt, x_sample, c_prompt, c_sample, state_conv, state_gla, cache_k, cache_v, w_ada, b_ada, g_mix, g_ffn, w_in_ab, conv_w, conv_b, conv_ln_g, conv_ln_b, gla_w_gate, gla_b_gate, gla_g_norm, w_in_cd, sgu_w, sgu_b, sgu_ln_g, sgu_ln_b, att_sinks, w_out, w_ffn_gate, w_ffn_up, w_ffn_down, g_final):
    nb, seq, d = x_prompt.shape
    ndb, dseq, _ = x_sample.shape
    ns = ndb * dseq
    assert d == D_MODEL and dseq <= SAMPLE_ROWS and seq % DENSE_ROWS == 0 and ns % PROJ_ROWS == 0
    sb = ROW_BLOCK // SAMPLE_ROWS
    assert ndb % sb == 0

    w_ab = jnp.pad(w_in_ab, ((0, 0), (0, IN_AB_PAD - IN_AB))).astype(BF16)
    w_cd = w_in_cd.astype(BF16)
    w_o = w_out.astype(BF16)
    w_fg, w_fu, w_fd = w_ffn_gate.astype(BF16), w_ffn_up.astype(BF16), w_ffn_down.astype(BF16)

    tri = jnp.tril(jnp.ones((ROW_BLOCK, ROW_BLOCK), bool))
    sgu_wp = jnp.where(tri, sgu_w, 0.0).astype(BF16)
    sgu_ws = jnp.where(tri, sgu_w, 0.0)[:, :SAMPLE_ROWS, :SAMPLE_ROWS]
    sgu_ws = jnp.einsum("ab,gts->gatbs", jnp.eye(sb, dtype=F32), sgu_ws).reshape(
        SGU_GROUPS, ROW_BLOCK, ROW_BLOCK).astype(BF16)
    sgu_bias_p = jnp.repeat(sgu_b.T, SGU_GD, axis=1)
    sgu_bias_s = jnp.tile(sgu_bias_p[:SAMPLE_ROWS], (sb, 1))

    c_all = jnp.concatenate([jnp.repeat(c_sample, dseq, axis=0), c_prompt], axis=0)
    mods_rows, mods_p = _ada(c_all, ns, w_ada, b_ada)
    mods_s = mods_rows.reshape(mods_rows.shape[0], 1, ns, N_MODS * D_MODEL)

    rp_dense = _Rows(nb, seq, 1, DENSE_ROWS)
    rp_ffn = _Rows(nb, seq, 1, min(seq, FFN_ROWS))
    rs_dense = _Rows(1, ns, 1, min(ns, DENSE_ROWS), per_row=True)
    rp_in = _Rows(nb, seq, 1, PROJ_ROWS)
    rs_in = _Rows(1, ns, 1, PROJ_ROWS, per_row=True)
    rp_mix = _Rows(nb, seq, 1, PROJ_ROWS)
    rp_att = _Rows(nb, seq, 1, WINDOW)
    rs_mix = _Rows(ndb, SAMPLE_ROWS, sb, SAMPLE_ROWS)

    def to_mixer(z):
        z = z.reshape(ndb, dseq, z.shape[-1])
        return jnp.pad(z, ((0, 0), (0, SAMPLE_ROWS - dseq), (0, 0)))

    def to_dense(o):
        return o[:, :dseq].reshape(1, ns, o.shape[-1])

    yp, ys = x_prompt, x_sample.reshape(1, ns, d)

    conv0 = jnp.zeros((nb, CONV_HIST, D_CONV), F32)
    zp, oa_p, st_conv_p = _inproj_conv(rp_in, yp, mods_p, 0, g_mix, w_ab, conv0, conv_w, conv_b,
                                       conv_ln_g, conv_ln_b, "inproj_conv_p")
    zs = to_mixer(_inproj(rs_in, ys, mods_s, 0, g_mix, w_ab, "inproj_ab_s"))
    oa_s, st_conv_s = _conv(rs_mix, zs, state_conv, conv_w, conv_b, conv_ln_g, conv_ln_b, dseq, "conv_s")
    ob_p, st_gla_p = _gla(rp_mix, zp, 0, None, gla_w_gate, gla_b_gate, gla_g_norm, ROW_BLOCK, "gla_p")
    ob_s, st_gla_s = _gla(rs_mix, zs, 2 * D_CONV, state_gla, gla_w_gate, gla_b_gate, gla_g_norm, dseq, "gla_s")
    yp = _outproj(rp_dense, oa_p, ob_p, yp, mods_p, 0, w_o, "outproj0_p")
    ys = _outproj(rs_dense, to_dense(oa_s), to_dense(ob_s), ys, mods_s, 0, w_o, "outproj0_s")
    yp = _ffn(rp_ffn, yp, mods_p, 0, g_ffn, w_fg, w_fu, w_fd, g_final, False, "ffn0_p")
    ys = _ffn(rs_dense, ys, mods_s, 0, g_ffn, w_fg, w_fu, w_fd, g_final, False, "ffn0_s")

    zp, oc_p = _inproj_sgu(rp_in, yp, mods_p, 1, g_mix, w_cd, sgu_wp, sgu_bias_p, sgu_ln_g, sgu_ln_b,
                           "inproj_sgu_p")
    zs = to_mixer(_inproj(rs_in, ys, mods_s, 1, g_mix, w_cd, "inproj_cd_s"))
    oc_s, sgu_v = _sgu(rs_mix, zs, sgu_ws, sgu_bias_s, sgu_ln_g, sgu_ln_b, True, "sgu_s")
    wb = cache_k.shape[1]
    od_p = _swa(rp_att, zp, 0, att_sinks, None, None, "swa_p")
    od_s = _swa(rs_mix, zs, 2 * D_SGU, att_sinks, cache_k.reshape(ndb, wb, KV_WIDTH),
                cache_v.reshape(ndb, wb, KV_WIDTH), "swa_s")
    yp = _outproj(rp_dense, oc_p, od_p, yp, mods_p, 1, w_o, "outproj1_p")
    ys = _outproj(rs_dense, to_dense(oc_s), to_dense(od_s), ys, mods_s, 1, w_o, "outproj1_s")
    yp = _ffn(rp_ffn, yp, mods_p, 1, g_ffn, w_fg, w_fu, w_fd, g_final, True, "ffn1_p")
    ys = _ffn(rs_dense, ys, mods_s, 1, g_ffn, w_fg, w_fu, w_fd, g_final, True, "ffn1_s")

    k0 = 2 * D_SGU + ATT_WIDTH
    kv_shape = (ATT_KV_HEADS, ATT_HD)
    kp0 = ATT_WIDTH
    k_new_p = zp[:, seq - wb:, kp0:kp0 + KV_WIDTH].reshape(nb, wb, *kv_shape)
    v_new_p = zp[:, seq - wb:, kp0 + KV_WIDTH:kp0 + 2 * KV_WIDTH].reshape(nb, wb, *kv_shape)
    k_new_s = zs[:, :dseq, k0:k0 + KV_WIDTH].reshape(ndb, dseq, *kv_shape)
    v_new_s = zs[:, :dseq, k0 + KV_WIDTH:k0 + 2 * KV_WIDTH].reshape(ndb, dseq, *kv_shape)
    k_s = jnp.concatenate([cache_k, k_new_s], axis=1)[:, -wb:]
    v_s = jnp.concatenate([cache_v, v_new_s], axis=1)[:, -wb:]
    return (yp, ys.reshape(ndb, dseq, d), st_conv_p, st_conv_s, st_gla_p, st_gla_s, sgu_v[:, :dseq],
            k_new_p, k_s, v_new_p, v_s)
```
